```python
import math
import jax
import jax.numpy as jnp
from jax import lax
import numpy as np

D_MODEL = 4096
BATCH = 1
SEQ = 16384
DEPTH = 1
DEC_BATCH = 16
DEC_SEQ = 16
PAST_LEN = 4096

CHUNK = 64
Q_BLOCK = 128
N_HEADS = 16
HEAD_DIM = 128
ATTN_DIM = N_HEADS * 2 * HEAD_DIM
CONV_DIM = 2048
CONV_WIDTH = 31
MEM_LEN = 256
MEM_HEADS = 4
MEM_QK_DIM = 128
MEM_V_DIM = 512
MEM_DIM = MEM_HEADS * MEM_V_DIM
N_BRANCH = 3
D_FF = 11008
REL_BUCKETS = 32
REL_MAX_DIST = 128
EPS = 1e-6
NEG_INF = -1e30

C_Q = ATTN_DIM
C_K = 2 * ATTN_DIM
C_V = 3 * ATTN_DIM
C_CONV = C_V + 2 * CONV_DIM
C_MEMQ = C_CONV + MEM_HEADS * MEM_QK_DIM
IN_COLS = C_MEMQ + N_BRANCH * D_MODEL

kernel_name = 'hybrid_stream_encoder_step'


def rmsnorm(x, g):
    xf = x.astype(jnp.float32)
    y = xf * lax.rsqrt(jnp.mean(xf * xf, axis=-1, keepdims=True) + EPS)
    return (y * g.astype(jnp.float32)).astype(x.dtype)


def layernorm(x, g, b):
    xf = x.astype(jnp.float32)
    xc = xf - jnp.mean(xf, axis=-1, keepdims=True)
    y = xc * lax.rsqrt(jnp.mean(xc * xc, axis=-1, keepdims=True) + EPS)
    return (y * g.astype(jnp.float32) + b.astype(jnp.float32)).astype(x.dtype)


def swiglu_ffn(x, w_gate, w_up, w_down):
    return (jax.nn.silu(x @ w_gate) * (x @ w_up)) @ w_down


def t5_bucket(rel):
    nb = REL_BUCKETS // 2
    max_exact = nb // 2
    n = jnp.abs(rel)
    nf = jnp.maximum(n, max_exact).astype(jnp.float32)
    large = max_exact + (jnp.log(nf / max_exact) / math.log(REL_MAX_DIST / max_exact)
                         * (nb - max_exact)).astype(jnp.int32)
    large = jnp.minimum(large, nb - 1)
    return jnp.where(rel > 0, nb, 0) + jnp.where(n < max_exact, n, large)


def rel_bias(qpos, kpos, table):
    b = t5_bucket(kpos[None, :] - qpos[:, None])
    return jnp.transpose(table[b], (2, 0, 1)).astype(jnp.float32)


def diff_attention(q1, q2, k1, k2, v, qpos, kpos, lam, table):
    scale = HEAD_DIM ** -0.5
    bias = rel_bias(qpos, kpos, table)[None]
    visible = (kpos[None, :] // CHUNK) <= (qpos[:, None] // CHUNK)

    def attn_map(q, k):
        s = jnp.einsum('bqhd,bkhd->bhqk', q, k).astype(jnp.float32) * scale + bias
        return jax.nn.softmax(jnp.where(visible, s, NEG_INF), axis=-1)

    a = attn_map(q1, k1) - lam * attn_map(q2, k2)
    return jnp.einsum('bhqk,bkhe->bqhe', a.astype(v.dtype), v)


def diff_attention_blocked(q1, q2, k1, k2, v, lam, table):
    B, T = q1.shape[:2]
    nb = T // Q_BLOCK
    pos = jnp.arange(T, dtype=jnp.int32)

    def to_blocks(q):
        return q.reshape(B, nb, Q_BLOCK, N_HEADS, HEAD_DIM).swapaxes(0, 1)

    def one_block(args):
        qb1, qb2, qpos = args
        return diff_attention(qb1, qb2, k1, k2, v, qpos, pos, lam, table)

    o = lax.map(one_block, (to_blocks(q1), to_blocks(q2), pos.reshape(nb, Q_BLOCK)))
    return o.swapaxes(0, 1).reshape(B, T, N_HEADS, 2 * HEAD_DIM)


def causal_depthwise(upad, w, b):
    y = lax.conv_general_dilated(upad, w[:, None, :], window_strides=(1,), padding='VALID',
                                 dimension_numbers=('NWC', 'WIO', 'NWC'),
                                 feature_group_count=CONV_DIM)
    return y + b


def memory_kv(mem, mem_norm, w_mem_kv, mem_k_norm):
    B, M, _ = mem.shape
    kv = rmsnorm(mem, mem_norm) @ w_mem_kv
    k = rmsnorm(kv[..., :MEM_HEADS * MEM_QK_DIM].reshape(B, M, MEM_HEADS, MEM_QK_DIM), mem_k_norm)
    v = kv[..., MEM_HEADS * MEM_QK_DIM:].reshape(B, M, MEM_HEADS, MEM_V_DIM)
    return k, v


def mix_block(h, past, mem_k, mem_v, table, lam_init, p):
    B, T, _ = h.shape
    z = h @ p['w_in']
    q = rmsnorm(z[..., :C_Q].reshape(B, T, N_HEADS, 2, HEAD_DIM), p['attn_q_norm'])
    k = rmsnorm(z[..., C_Q:C_K].reshape(B, T, N_HEADS, 2, HEAD_DIM), p['attn_k_norm'])
    k_rows = k.reshape(B, T, N_HEADS, 2 * HEAD_DIM)
    v_rows = z[..., C_K:C_V].reshape(B, T, N_HEADS, 2 * HEAD_DIM)
    lam = (jnp.exp(jnp.sum(p['lambda_q1'].astype(jnp.float32) * p['lambda_k1'].astype(jnp.float32)))
           - jnp.exp(jnp.sum(p['lambda_q2'].astype(jnp.float32) * p['lambda_k2'].astype(jnp.float32)))
           + lam_init)
    if past is None:
        o = diff_attention_blocked(q[..., 0, :], q[..., 1, :], k[..., 0, :], k[..., 1, :],
                                   v_rows, lam, table)
        conv_hist = jnp.zeros((B, CONV_WIDTH - 1, CONV_DIM), h.dtype)
    else:
        cache_k, cache_v, conv_hist = past
        P = cache_k.shape[1]
        k_all = jnp.concatenate([cache_k, k_rows], axis=1)
        v_all = jnp.concatenate([cache_v, v_rows], axis=1)
        qpos = P + jnp.arange(T, dtype=jnp.int32)
        kpos = jnp.arange(P + T, dtype=jnp.int32)
        o = diff_attention(q[..., 0, :], q[..., 1, :], k_all[..., :HEAD_DIM], k_all[..., HEAD_DIM:],
                           v_all, qpos, kpos, lam, table)
    o = rmsnorm(o, p['attn_subln']) * (1.0 - lam_init)
    y_attn = o.reshape(B, T, ATTN_DIM) @ p['w_attn_out']
    u = z[..., C_V:C_V + CONV_DIM] * jax.nn.sigmoid(z[..., C_V + CONV_DIM:C_CONV])
    upad = jnp.concatenate([conv_hist, u], axis=1)
    c = causal_depthwise(upad, p['conv_dw'], p['conv_dw_bias'])
    c = jax.nn.silu(layernorm(c, p['conv_ln_g'], p['conv_ln_b']))
    y_conv = c @ p['w_conv_out']
    conv_state = upad[:, -(CONV_WIDTH - 1):]
    mq = rmsnorm(z[..., C_CONV:C_MEMQ].reshape(B, T, MEM_HEADS, MEM_QK_DIM), p['mem_q_norm'])
    s = jnp.einsum('bqhd,bmhd->bhqm', mq, mem_k).astype(jnp.float32) * (MEM_QK_DIM ** -0.5)
    pm = jax.nn.softmax(s, axis=-1).astype(mem_v.dtype)
    y_mem = jnp.einsum('bhqm,bmhe->bqhe', pm, mem_v).reshape(B, T, MEM_DIM) @ p['w_mem_out']
    g = jax.nn.sigmoid(z[..., C_MEMQ:].reshape(B, T, N_BRANCH, D_MODEL))
    merged = g[..., 0, :] * y_attn + g[..., 1, :] * y_conv + g[..., 2, :] * y_mem
    return merged @ p['w_out'], k_rows, v_rows, conv_state


def encoder_layer(x, past, mem_k, mem_v, table, lam_init, p):
    x = x + 0.5 * swiglu_ffn(rmsnorm(x, p['ffn1_norm']), p['ffn1_w_gate'], p['ffn1_w_up'], p['ffn1_w_down'])
    m, k_rows, v_rows, conv_state = mix_block(rmsnorm(x, p['mix_norm']), past, mem_k, mem_v,
                                              table, lam_init, p)
    x = x + m
    x = x + 0.5 * swiglu_ffn(rmsnorm(x, p['ffn2_norm']), p['ffn2_w_gate'], p['ffn2_w_up'], p['ffn2_w_down'])
    return x, k_rows, v_rows, conv_state


def setup_inputs(seed: int = 0) -> dict:
    key = jax.random.key(seed)
    ks = iter(jax.random.split(key, 48))

    def nrm(shape, scale):
        return jax.random.normal(next(ks), shape, jnp.float32) * scale

    def gain(n):
        return 1.0 + nrm((DEPTH, n), 0.05)

    L = DEPTH
    return {
        'x_prompt': nrm((BATCH, SEQ, D_MODEL), 1.0),
        'x_sample': nrm((DEC_BATCH, DEC_SEQ, D_MODEL), 1.0),
        'cache_attn_k': nrm((L, DEC_BATCH, PAST_LEN, N_HEADS, 2 * HEAD_DIM), 1.0),
        'cache_attn_v': nrm((L, DEC_BATCH, PAST_LEN, N_HEADS, 2 * HEAD_DIM), 1.0),
        'cache_conv': nrm((L, DEC_BATCH, CONV_WIDTH - 1, CONV_DIM), 0.5),
        'cache_mem_k': nrm((L, DEC_BATCH, MEM_LEN, MEM_HEADS, MEM_QK_DIM), 1.0),
        'cache_mem_v': nrm((L, DEC_BATCH, MEM_LEN, MEM_HEADS, MEM_V_DIM), 1.0),
        'mem_prompt': nrm((BATCH, MEM_LEN, D_MODEL), 1.0),
        'rel_bias_table': nrm((REL_BUCKETS, N_HEADS), 0.2),
        'ffn1_norm': gain(D_MODEL),
        'ffn1_w_gate': nrm((L, D_MODEL, D_FF), D_MODEL ** -0.5),
        'ffn1_w_up': nrm((L, D_MODEL, D_FF), D_MODEL ** -0.5),
        'ffn1_w_down': nrm((L, D_FF, D_MODEL), D_FF ** -0.5),
        'mix_norm': gain(D_MODEL),
        'w_in': nrm((L, D_MODEL, IN_COLS), D_MODEL ** -0.5),
        'attn_q_norm': gain(HEAD_DIM),
        'attn_k_norm': gain(HEAD_DIM),
        'lambda_q1': nrm((L, HEAD_DIM), 0.1),
        'lambda_k1': nrm((L, HEAD_DIM), 0.1),
        'lambda_q2': nrm((L, HEAD_DIM), 0.1),
        'lambda_k2': nrm((L, HEAD_DIM), 0.1),
        'attn_subln': gain(2 * HEAD_DIM),
        'w_attn_out': nrm((L, ATTN_DIM, D_MODEL), ATTN_DIM ** -0.5),
        'conv_dw': nrm((L, CONV_WIDTH, CONV_DIM), CONV_WIDTH ** -0.5),
        'conv_dw_bias': nrm((L, CONV_DIM), 0.02),
        'conv_ln_g': gain(CONV_DIM),
        'conv_ln_b': nrm((L, CONV_DIM), 0.02),
        'w_conv_out': nrm((L, CONV_DIM, D_MODEL), CONV_DIM ** -0.5),
        'mem_norm': gain(D_MODEL),
        'w_mem_kv': nrm((L, D_MODEL, MEM_HEADS * (MEM_QK_DIM + MEM_V_DIM)), D_MODEL ** -0.5),
        'mem_q_norm': gain(MEM_QK_DIM),
        'mem_k_norm': gain(MEM_QK_DIM),
        'w_mem_out': nrm((L, MEM_DIM, D_MODEL), MEM_DIM ** -0.5),
        'w_out': nrm((L, D_MODEL, D_MODEL), D_MODEL ** -0.5),
        'ffn2_norm': gain(D_MODEL),
        'ffn2_w_gate': nrm((L, D_MODEL, D_FF), D_MODEL ** -0.5),
        'ffn2_w_up': nrm((L, D_MODEL, D_FF), D_MODEL ** -0.5),
        'ffn2_w_down': nrm((L, D_FF, D_MODEL), D_FF ** -0.5),
    }


def reference(x_prompt, x_sample, cache_attn_k, cache_attn_v, cache_conv, cache_mem_k, cache_mem_v,
              mem_prompt, rel_bias_table, ffn1_norm, ffn1_w_gate, ffn1_w_up, ffn1_w_down, mix_norm,
              w_in, attn_q_norm, attn_k_norm, lambda_q1, lambda_k1, lambda_q2, lambda_k2, attn_subln,
              w_attn_out, conv_dw, conv_dw_bias, conv_ln_g, conv_ln_b, w_conv_out, mem_norm, w_mem_kv,
              mem_q_norm, mem_k_norm, w_mem_out, w_out, ffn2_norm, ffn2_w_gate, ffn2_w_up, ffn2_w_down):
    yp, ys = x_prompt, x_sample
    kp_l, vp_l, cp_l, mkp_l, mvp_l, ks_l, vs_l, cs_l = [], [], [], [], [], [], [], []
    for l in range(DEPTH):
        p = {
            'ffn1_norm': ffn1_norm[l], 'ffn1_w_gate': ffn1_w_gate[l], 'ffn1_w_up': ffn1_w_up[l],
            'ffn1_w_down': ffn1_w_down[l], 'mix_norm': mix_norm[l], 'w_in': w_in[l],
            'attn_q_norm': attn_q_norm[l], 'attn_k_norm': attn_k_norm[l],
            'lambda_q1': lambda_q1[l], 'lambda_k1': lambda_k1[l],
            'lambda_q2': lambda_q2[l], 'lambda_k2': lambda_k2[l],
            'attn_subln': attn_subln[l], 'w_attn_out': w_attn_out[l],
            'conv_dw': conv_dw[l], 'conv_dw_bias': conv_dw_bias[l],
            'conv_ln_g': conv_ln_g[l], 'conv_ln_b': conv_ln_b[l], 'w_conv_out': w_conv_out[l],
            'mem_q_norm': mem_q_norm[l], 'w_mem_out': w_mem_out[l], 'w_out': w_out[l],
            'ffn2_norm': ffn2_norm[l], 'ffn2_w_gate': ffn2_w_gate[l], 'ffn2_w_up': ffn2_w_up[l],
            'ffn2_w_down': ffn2_w_down[l],
        }
        lam_init = 0.8 - 0.6 * math.exp(-0.3 * l)
        mk, mv = memory_kv(mem_prompt, mem_norm[l], w_mem_kv[l], mem_k_norm[l])
        yp, kp, vp, cp = encoder_layer(yp, None, mk, mv, rel_bias_table, lam_init, p)
        ys, ks, vs, cs = encoder_layer(ys, (cache_attn_k[l], cache_attn_v[l], cache_conv[l]),
                                       cache_mem_k[l], cache_mem_v[l], rel_bias_table, lam_init, p)
        kp_l.append(kp); vp_l.append(vp); cp_l.append(cp); mkp_l.append(mk); mvp_l.append(mv)
        ks_l.append(ks); vs_l.append(vs); cs_l.append(cs)
    return (yp, ys, jnp.stack(kp_l), jnp.stack(vp_l), jnp.stack(cp_l), jnp.stack(mkp_l),
            jnp.stack(mvp_l), jnp.stack(ks_l), jnp.stack(vs_l), jnp.stack(cs_l))
```

```python
import functools
import math

import numpy as np
import jax
import jax.numpy as jnp
from jax import lax
from jax.experimental import pallas as pl
from jax.experimental.pallas import tpu as pltpu

F32 = jnp.float32
BF16 = jnp.bfloat16

EPS = 1e-6
NEG_INF = -1e30
CHUNK = 64
REL_BUCKETS = 32
REL_MAX_DIST = 128
LOG2E = math.log2(math.e)

V7X_VMEM_LIMIT_MAX = 56 * 1024 * 1024
V7X_LANES = 128
V7X_MXU_DIM = 256
CONV_HALO = 32


def _cparams(n_axes, vmem_bytes):
    limit = min(max(int(vmem_bytes * 1.2) + (6 << 20), 32 << 20), V7X_VMEM_LIMIT_MAX)
    return pltpu.CompilerParams(dimension_semantics=("arbitrary",) * n_axes,
                                vmem_limit_bytes=limit)


def _tile(n, pref):
    if n <= pref:
        return n
    t = pref
    while n % t:
        t //= 2
    return t


def _rms(x, g):
    return x * lax.rsqrt(jnp.mean(x * x, axis=-1, keepdims=True) + EPS) * g


def _group_rms(x, g, width):
    parts = []
    for c in range(0, x.shape[-1], width):
        parts.append(_rms(x[:, c:c + width], g[:, c:c + width]))
    return parts[0] if len(parts) == 1 else jnp.concatenate(parts, axis=-1)


def _rmsnorm_body(x_ref, g_ref, o_ref):
    o_ref[...] = _rms(x_ref[...], g_ref[...]).astype(o_ref.dtype)


def _rmsnorm(x, g):
    R, D = x.shape
    tm = _tile(R, 256)
    return pl.pallas_call(
        _rmsnorm_body,
        grid=(R // tm,),
        in_specs=[pl.BlockSpec((tm, D), lambda i: (i, 0)),
                  pl.BlockSpec((1, D), lambda i: (0, 0))],
        out_specs=pl.BlockSpec((tm, D), lambda i: (i, 0)),
        out_shape=jax.ShapeDtypeStruct((R, D), BF16),
        compiler_params=_cparams(1, 2 * tm * D * 6),
        name="rmsnorm",
    )(x, g.reshape(1, D))


def _ffn_body(emit_norm, nj, x_ref, g_ref, wg_ref, wu_ref, wd_ref, *rest):
    if emit_norm:
        g2_ref, out_ref, hn_ref, xn_scr = rest
    else:
        out_ref, xn_scr = rest
    j = pl.program_id(1)

    tm = out_ref.shape[0]
    rc = min(tm, 64)

    @pl.when(j == 0)
    def _():
        for r0 in range(0, tm, rc):
            x = x_ref[r0:r0 + rc, :]
            xn_scr[r0:r0 + rc, :] = _rms(x, g_ref[...]).astype(BF16)
            out_ref[r0:r0 + rc, :] = x

    xn = xn_scr[...]
    a = jnp.dot(xn, wg_ref[...], preferred_element_type=F32)
    b = jnp.dot(xn, wu_ref[...], preferred_element_type=F32)
    h = (a * jax.nn.sigmoid(a) * b * 0.5).astype(BF16)
    D = out_ref.shape[1]
    cw = min(D, 1024)
    for c0 in range(0, D, cw):
        out_ref[:, c0:c0 + cw] += jnp.dot(h, wd_ref[:, c0:c0 + cw], preferred_element_type=F32)

    if emit_norm:
        @pl.when(j == nj - 1)
        def _():
            for r0 in range(0, tm, rc):
                hn_ref[r0:r0 + rc, :] = _rms(out_ref[r0:r0 + rc, :], g2_ref[...]).astype(BF16)


def _ffn(x, g, wg, wu, wd, g2=None):
    R, D = x.shape
    Fdim = wg.shape[1]
    tm = _tile(R, 512)
    tf = _tile(Fdim, V7X_MXU_DIM)
    nj = Fdim // tf
    emit_norm = g2 is not None
    in_specs = [pl.BlockSpec((tm, D), lambda i, j: (i, 0), pipeline_mode=pl.Buffered(1)),
                pl.BlockSpec((1, D), lambda i, j: (0, 0)),
                pl.BlockSpec((D, tf), lambda i, j: (0, j)),
                pl.BlockSpec((D, tf), lambda i, j: (0, j)),
                pl.BlockSpec((tf, D), lambda i, j: (j, 0))]
    args = [x, g.reshape(1, D), wg, wu, wd]
    out_specs = [pl.BlockSpec((tm, D), lambda i, j: (i, 0))]
    out_shape = [jax.ShapeDtypeStruct((R, D), F32)]
    vmem = tm * D * 4 + 2 * tm * D * 4 + tm * D * 2 + 2 * 3 * D * tf * 2 + 4 * tm * tf * 4
    if emit_norm:
        in_specs.append(pl.BlockSpec((1, D), lambda i, j: (0, 0)))
        args.append(g2.reshape(1, D))
        out_specs.append(pl.BlockSpec((tm, D), lambda i, j: (i, 0)))
        out_shape.append(jax.ShapeDtypeStruct((R, D), BF16))
        vmem += 2 * tm * D * 2
    res = pl.pallas_call(
        functools.partial(_ffn_body, emit_norm, nj),
        grid=(R // tm, nj),
        in_specs=in_specs,
        out_specs=out_specs,
        out_shape=out_shape,
        scratch_shapes=[pltpu.VMEM((tm, D), BF16)],
        compiler_params=_cparams(2, vmem),
        name="ffn_norm" if emit_norm else "ffn",
    )(*args)
    return res if emit_norm else res[0]


def _proj_body(epi, n_w, n_ex, a_ref, *refs):
    w_refs = refs[:n_w]
    ex_refs = refs[n_w:n_w + n_ex]
    out_refs = refs[n_w + n_ex:]
    a = a_ref[...]
    accs = [jnp.dot(a, w[...], preferred_element_type=F32) for w in w_refs]
    y = epi(accs, [e[...] for e in ex_refs])
    for o in out_refs:
        o[...] = y.astype(o.dtype)


def _proj(a, ws, epi, out_dtypes, *, cols=(), tiles=(), name, tm_pref=1024, tn_pref=512):
    R, K = a.shape
    N = ws[0].shape[1]
    tm = _tile(R, tm_pref)
    tn = _tile(N, tn_pref)
    in_specs = [pl.BlockSpec((tm, K), lambda i, j: (i, 0))]
    in_specs += [pl.BlockSpec((K, tn), lambda i, j: (0, j)) for _ in ws]
    in_specs += [pl.BlockSpec((1, tn), lambda i, j: (0, j)) for _ in cols]
    in_specs += [pl.BlockSpec((tm, tn), lambda i, j: (i, j)) for _ in tiles]
    vmem = 2 * tm * K * 2 + 2 * len(ws) * K * tn * 2 + (len(ws) + 2) * tm * tn * 4
    vmem += sum(2 * tm * tn * t.dtype.itemsize for t in tiles)
    vmem += sum(2 * tm * tn * jnp.dtype(d).itemsize for d in out_dtypes)
    res = pl.pallas_call(
        functools.partial(_proj_body, epi, len(ws), len(cols) + len(tiles)),
        grid=(R // tm, N // tn),
        in_specs=in_specs,
        out_specs=[pl.BlockSpec((tm, tn), lambda i, j: (i, j)) for _ in out_dtypes],
        out_shape=[jax.ShapeDtypeStruct((R, N), d) for d in out_dtypes],
        compiler_params=_cparams(2, vmem),
        name=name,
    )(a, *ws, *cols, *tiles)
    return res


def _epi_plain(accs, ex):
    return accs[0]


def _epi_group_rms(width, accs, ex):
    return _group_rms(accs[0], ex[0], width)


def _epi_glu(accs, ex):
    return accs[0] * jax.nn.sigmoid(accs[1])


def _epi_sigmoid(accs, ex):
    return jax.nn.sigmoid(accs[0])


def _epi_residual(accs, ex):
    return ex[0] + accs[0]


def _merge_body(a_ref, c_ref, m_ref, wa_ref, wc_ref, wm_ref, ga_ref, gc_ref, gm_ref, o_ref):
    ya = jnp.dot(a_ref[...], wa_ref[...], preferred_element_type=F32)
    yc = jnp.dot(c_ref[...], wc_ref[...], preferred_element_type=F32)
    ym = jnp.dot(m_ref[...], wm_ref[...], preferred_element_type=F32)
    merged = ga_ref[...].astype(F32) * ya + gc_ref[...].astype(F32) * yc + gm_ref[...].astype(F32) * ym
    o_ref[...] = merged.astype(o_ref.dtype)


def _merge(oa, oc, om, wa, wc, wm, gates):
    R = oa.shape[0]
    D = wa.shape[1]
    tm = _tile(R, 512)
    tn = _tile(D, 512)
    nt = D // tn
    row = lambda arr: pl.BlockSpec((tm, arr.shape[1]), lambda i, j: (i, 0))
    col = lambda arr: pl.BlockSpec((arr.shape[0], tn), lambda i, j: (0, j))
    gate = lambda b: pl.BlockSpec((tm, tn), lambda i, j: (i, b * nt + j))
    ksum = oa.shape[1] + oc.shape[1] + om.shape[1]
    vmem = 2 * tm * ksum * 2 + 2 * ksum * tn * 2 + 2 * 4 * tm * tn * 2 + 4 * tm * tn * 4
    return pl.pallas_call(
        _merge_body,
        grid=(R // tm, nt),
        in_specs=[row(oa), row(oc), row(om), col(wa), col(wc), col(wm), gate(0), gate(1), gate(2)],
        out_specs=pl.BlockSpec((tm, tn), lambda i, j: (i, j)),
        out_shape=jax.ShapeDtypeStruct((R, D), BF16),
        compiler_params=_cparams(2, vmem),
        name="merge",
    )(oa, oc, om, wa, wc, wm, gates, gates, gates)


def _t5_bucket(rel):
    nb = REL_BUCKETS // 2
    max_exact = nb // 2
    n = jnp.abs(rel)
    nf = jnp.maximum(n, max_exact).astype(F32)
    large = max_exact + (jnp.log(nf / max_exact) / math.log(REL_MAX_DIST / max_exact)
                         * (nb - max_exact)).astype(jnp.int32)
    large = jnp.minimum(large, nb - 1)
    return jnp.where(rel > 0, nb, 0) + jnp.where(n < max_exact, n, large)


def _bias_window(table, qpos, kpos):
    qpos = np.asarray(qpos, np.int32)
    kpos = np.asarray(kpos, np.int32)
    rel = jnp.asarray(kpos[None, :] - qpos[:, None])
    b = jnp.transpose(table[_t5_bucket(rel)], (2, 0, 1)).astype(F32)
    far = table[REL_BUCKETS // 2 - 1].astype(F32)[:, None, None]
    visible = jnp.asarray((kpos[None, :] // CHUNK) <= (qpos[:, None] // CHUNK))
    return jnp.where(visible[None], (b - far) * LOG2E, NEG_INF)


def _lambda(lam_ref, lam_init):
    lv = lam_ref[...]
    s1 = jnp.sum(lv[0:1] * lv[1:2], axis=-1, keepdims=True)
    s2 = jnp.sum(lv[2:3] * lv[3:4], axis=-1, keepdims=True)
    return jnp.exp(s1) - jnp.exp(s2) + lam_init


def _attn_prompt_body(t, d, lam_init, q_ref, k_ref, v_ref, bias_ref, lam_ref, g_ref, o_ref,
                      m_scr, l_scr, acc_scr):
    i = pl.program_id(1)
    m_scr[...] = jnp.full(m_scr.shape, NEG_INF, F32)
    l_scr[...] = jnp.zeros(l_scr.shape, F32)
    acc_scr[...] = jnp.zeros(acc_scr.shape, F32)

    def step(row0, bias):
        vt = v_ref[pl.ds(row0, t), :]
        for s in range(2):
            sc = lax.dot_general(q_ref[:, s * d:(s + 1) * d], k_ref[pl.ds(row0, t), s * d:(s + 1) * d],
                                 (((1,), (1,)), ((), ())), preferred_element_type=F32)
            if bias is not None:
                sc = sc + bias
            m_old = m_scr[s]
            m_new = jnp.maximum(m_old, jnp.max(sc, axis=-1, keepdims=True))
            alpha = jnp.exp2(m_old - m_new)
            p = jnp.exp2(sc - m_new)
            l_scr[s] = alpha * l_scr[s] + jnp.sum(p, axis=-1, keepdims=True)
            acc_scr[s] = alpha * acc_scr[s] + jnp.dot(p.astype(BF16), vt, preferred_element_type=F32)
            m_scr[s] = m_new

    def far(j, carry):
        step(pl.multiple_of(j * t, t), None)
        return carry

    lax.fori_loop(0, jnp.maximum(i - 1, 0), far, 0)

    @pl.when(i > 0)
    def _():
        step(pl.multiple_of((i - 1) * t, t), bias_ref[0, 1])

    step(pl.multiple_of(i * t, t), bias_ref[0, 0])

    lam = _lambda(lam_ref, lam_init)
    o = acc_scr[0] / l_scr[0] - lam * (acc_scr[1] / l_scr[1])
    o_ref[...] = (_rms(o, g_ref[...]) * (1.0 - lam_init)).astype(o_ref.dtype)


def _attn_prompt(q, k, v, table, lam_vecs, subln, lam_init, n_heads, t_pref=512):
    T = q.shape[0]
    w = q.shape[1] // n_heads
    d = w // 2
    t = _tile(T, t_pref)
    assert t % CHUNK == 0 and t >= REL_MAX_DIST, (t, "key tiles two back must be at constant bias")
    pos = np.arange(t)
    bias = jnp.stack([_bias_window(table, pos, pos), _bias_window(table, pos, pos - t)], axis=1)
    vmem = 2 * 2 * T * w * 2 + 2 * 2 * t * t * 4 + 4 * t * w * 2 + 2 * t * w * 4 + 6 * t * t * 4
    return pl.pallas_call(
        functools.partial(_attn_prompt_body, t, d, lam_init),
        grid=(n_heads, T // t),
        in_specs=[pl.BlockSpec((t, w), lambda h, i: (i, h)),
                  pl.BlockSpec((T, w), lambda h, i: (0, h)),
                  pl.BlockSpec((T, w), lambda h, i: (0, h)),
                  pl.BlockSpec((1, 2, t, t), lambda h, i: (h, 0, 0, 0)),
                  pl.BlockSpec((4, d), lambda h, i: (0, 0)),
                  pl.BlockSpec((1, w), lambda h, i: (0, 0))],
        out_specs=pl.BlockSpec((t, w), lambda h, i: (i, h)),
        out_shape=jax.ShapeDtypeStruct((T, n_heads * w), BF16),
        scratch_shapes=[pltpu.VMEM((2, t, 1), F32), pltpu.VMEM((2, t, 1), F32),
                        pltpu.VMEM((2, t, w), F32)],
        compiler_params=_cparams(2, vmem),
        name="attn_prompt",
    )(q, k, v, bias, lam_vecs, subln.reshape(1, w))


def _attn_sample_body(d, lam_init, q_ref, ck_ref, cv_ref, kn_ref, vn_ref, bc_ref, bn_ref, lam_ref, g_ref,
                      o_ref):
    kc = ck_ref[0].astype(BF16)
    vc = cv_ref[0].astype(BF16)
    kn = kn_ref[0]
    vn = vn_ref[0]
    q = q_ref[0]
    nt = (((1,), (1,)), ((), ()))
    outs = []
    for s in range(2):
        qs = q[:, s * d:(s + 1) * d]
        sc = lax.dot_general(qs, kc[:, s * d:(s + 1) * d], nt, preferred_element_type=F32) + bc_ref[0]
        sn = lax.dot_general(qs, kn[:, s * d:(s + 1) * d], nt, preferred_element_type=F32) + bn_ref[0]
        m = jnp.maximum(jnp.max(sc, axis=-1, keepdims=True), jnp.max(sn, axis=-1, keepdims=True))
        pc = jnp.exp2(sc - m)
        pn = jnp.exp2(sn - m)
        l = jnp.sum(pc, axis=-1, keepdims=True) + jnp.sum(pn, axis=-1, keepdims=True)
        acc = (jnp.dot(pc.astype(BF16), vc, preferred_element_type=F32)
               + jnp.dot(pn.astype(BF16), vn, preferred_element_type=F32))
        outs.append(acc / l)
    lam = _lambda(lam_ref, lam_init)
    o = outs[0] - lam * outs[1]
    o_ref[0] = (_rms(o, g_ref[...]) * (1.0 - lam_init)).astype(o_ref.dtype)


def _attn_sample(q, kn, vn, cache_k, cache_v, table, lam_vecs, subln, lam_init, n_heads):
    B, S, HW = q.shape
    P = cache_k.shape[1]
    w = HW // n_heads
    d = w // 2
    qpos = P + np.arange(S)
    bias_c = _bias_window(table, qpos, np.arange(P))
    bias_n = _bias_window(table, qpos, qpos)
    new = pl.BlockSpec((1, S, w), lambda b, h: (b, 0, h))
    old = pl.BlockSpec((1, P, w), lambda b, h: (b, 0, h))
    vmem = 2 * 2 * P * w * 4 + 2 * P * w * 2 + 2 * S * P * 4 + 8 * S * P * 4
    return pl.pallas_call(
        functools.partial(_attn_sample_body, d, lam_init),
        grid=(B, n_heads),
        in_specs=[new, old, old, new, new,
                  pl.BlockSpec((1, S, P), lambda b, h: (h, 0, 0)),
                  pl.BlockSpec((1, S, S), lambda b, h: (h, 0, 0)),
                  pl.BlockSpec((4, d), lambda b, h: (0, 0)),
                  pl.BlockSpec((1, w), lambda b, h: (0, 0))],
        out_specs=new,
        out_shape=jax.ShapeDtypeStruct((B, S, HW), BF16),
        compiler_params=_cparams(2, vmem),
        name="attn_sample",
    )(q, cache_k, cache_v, kn, vn, bias_c, bias_n, lam_vecs, subln.reshape(1, w))


def _conv_body(width, halo_ref, main_ref, w_ref, b_ref, g_ref, beta_ref, o_ref, win_scr, conv_scr):
    tm = main_ref.shape[1]
    C = main_ref.shape[2]
    win_scr[0:CONV_HALO, :] = halo_ref[0]
    win_scr[CONV_HALO:CONV_HALO + tm, :] = main_ref[0]
    rc = min(tm, 32)
    cc = min(C, 512)
    first = CONV_HALO - (width - 1)
    for r0 in range(0, tm, rc):
        for c0 in range(0, C, cc):
            acc = jnp.broadcast_to(b_ref[:, c0:c0 + cc], (rc, cc))
            for tap in range(width):
                acc = acc + win_scr[r0 + first + tap:r0 + first + tap + rc, c0:c0 + cc] * w_ref[tap:tap + 1, c0:c0 + cc]
            conv_scr[r0:r0 + rc, c0:c0 + cc] = acc
    c = conv_scr[...]
    xc = c - jnp.mean(c, axis=-1, keepdims=True)
    y = xc * lax.rsqrt(jnp.mean(xc * xc, axis=-1, keepdims=True) + EPS) * g_ref[...] + beta_ref[...]
    o_ref[0] = (y * jax.nn.sigmoid(y)).astype(o_ref.dtype)


def _conv_module(upad, pad, w, b, g, beta):
    B, TP, C = upad.shape
    T = TP - pad
    width = w.shape[0]
    tm = _tile(T, 64)
    assert pad % tm == 0 and pad >= CONV_HALO and (tm % CONV_HALO == 0 or tm == T)
    vec = lambda: pl.BlockSpec((1, C), lambda bb, i: (0, 0))
    return pl.pallas_call(
        functools.partial(_conv_body, width),
        grid=(B, T // tm),
        in_specs=[pl.BlockSpec((1, CONV_HALO, C), lambda bb, i: (bb, (pad + i * tm) // CONV_HALO - 1, 0)),
                  pl.BlockSpec((1, tm, C), lambda bb, i: (bb, pad // tm + i, 0)),
                  pl.BlockSpec((width, C), lambda bb, i: (0, 0)),
                  vec(), vec(), vec()],
        out_specs=pl.BlockSpec((1, tm, C), lambda bb, i: (bb, i, 0)),
        out_shape=jax.ShapeDtypeStruct((B, T, C), BF16),
        scratch_shapes=[pltpu.VMEM((CONV_HALO + tm, C), F32), pltpu.VMEM((tm, C), F32)],
        compiler_params=_cparams(2, 16 * (CONV_HALO + tm) * C),
        name="conv_module",
    )(upad, upad, w, b.reshape(1, C), g.reshape(1, C), beta.reshape(1, C))


def _mem_attn_body(n_heads, dq, dv, q_ref, k_ref, v_ref, o_ref):
    nt = (((1,), (1,)), ((), ()))
    outs = []
    for h in range(n_heads):
        qh = q_ref[0, :, h * dq:(h + 1) * dq]
        kh = k_ref[0, :, h * dq:(h + 1) * dq].astype(BF16)
        vh = v_ref[0, :, h * dv:(h + 1) * dv].astype(BF16)
        s = lax.dot_general(qh, kh, nt, preferred_element_type=F32)
        p = jnp.exp2(s - jnp.max(s, axis=-1, keepdims=True))
        l = jnp.sum(p, axis=-1, keepdims=True)
        outs.append(jnp.dot(p.astype(BF16), vh, preferred_element_type=F32) / l)
    o_ref[0] = jnp.concatenate(outs, axis=-1).astype(o_ref.dtype)


def _mem_attn(mq, mem_k, mem_v, n_heads):
    B, T, QW = mq.shape
    M = mem_k.shape[1]
    VW = mem_v.shape[2]
    tm = _tile(T, 512)
    return pl.pallas_call(
        functools.partial(_mem_attn_body, n_heads, QW // n_heads, VW // n_heads),
        grid=(B, T // tm),
        in_specs=[pl.BlockSpec((1, tm, QW), lambda b, i: (b, i, 0)),
                  pl.BlockSpec((1, M, QW), lambda b, i: (b, 0, 0)),
                  pl.BlockSpec((1, M, VW), lambda b, i: (b, 0, 0))],
        out_specs=pl.BlockSpec((1, tm, VW), lambda b, i: (b, i, 0)),
        out_shape=jax.ShapeDtypeStruct((B, T, VW), BF16),
        compiler_params=_cparams(2, 2 * (tm * QW * 2 + M * (QW + VW) * 4 + tm * VW * 2) + 3 * tm * VW * 4),
        name="mem_attn",
    )(mq, mem_k, mem_v)


def _layer_group(x, past, mem_k, mem_v, table, lam_init, p, dims):
    B, T, D = x.shape
    H, d, MH = dims["H"], dims["d"], dims["MH"]
    R = B * T
    xf = x.reshape(R, D)

    x1, h = _ffn(xf, p["ffn1_norm"], p["ffn1_w_gate"], p["ffn1_w_up"], p["ffn1_w_down"], p["mix_norm"])

    (q,) = _proj(h, [p["w_q"]], functools.partial(_epi_group_rms, d), [BF16], cols=[p["q_gain"]], name="proj_q")
    k32, k16 = _proj(h, [p["w_k"]], functools.partial(_epi_group_rms, d), [F32, BF16], cols=[p["k_gain"]],
                     name="proj_k")
    v32, v16 = _proj(h, [p["w_v"]], _epi_plain, [F32, BF16], name="proj_v")
    (u,) = _proj(h, [p["w_glu_a"], p["w_glu_b"]], _epi_glu, [F32], name="proj_glu")
    (mq,) = _proj(h, [p["w_mq"]], functools.partial(_epi_group_rms, dims["dmq"]), [BF16], cols=[p["mq_gain"]],
                  name="proj_mq")
    (gates,) = _proj(h, [p["w_gates"]], _epi_sigmoid, [BF16], name="proj_gates")

    HW = H * 2 * d
    C = u.shape[1]
    width = p["conv_dw"].shape[0]
    if past is None:
        assert B == 1
        oa = _attn_prompt(q, k16, v16, table, p["lam_vecs"], p["attn_subln"], lam_init, H)
        hist = jnp.zeros((B, width - 1, C), F32)
    else:
        cache_k, cache_v, hist = past
        P = cache_k.shape[1]
        oa = _attn_sample(q.reshape(B, T, HW), k16.reshape(B, T, HW), v16.reshape(B, T, HW),
                          cache_k.reshape(B, P, HW), cache_v.reshape(B, P, HW),
                          table, p["lam_vecs"], p["attn_subln"], lam_init, H).reshape(R, HW)

    tmc = _tile(T, 64)
    pad = max(tmc, CONV_HALO)
    upad = jnp.concatenate([jnp.zeros((B, pad - (width - 1), C), F32), hist, u.reshape(B, T, C)], axis=1)
    conv_state = upad[:, pad + T - (width - 1):]
    oc = _conv_module(upad, pad, p["conv_dw"], p["conv_dw_bias"], p["conv_ln_g"], p["conv_ln_b"]).reshape(R, C)

    M = mem_k.shape[1]
    om = _mem_attn(mq.reshape(B, T, -1), mem_k.reshape(B, M, -1), mem_v.reshape(B, M, -1), MH)
    om = om.reshape(R, -1)

    merged = _merge(oa, oc, om, p["w_attn_out"], p["w_conv_out"], p["w_mem_out"], gates)
    (x2,) = _proj(merged, [p["w_out"]], _epi_residual, [F32], tiles=[x1], name="proj_out")
    y = _ffn(x2, p["ffn2_norm"], p["ffn2_w_gate"], p["ffn2_w_up"], p["ffn2_w_down"])
    return (y.reshape(B, T, D), k32.reshape(B, T, H, 2 * d), v32.reshape(B, T, H, 2 * d), conv_state)


def _memory_kv(mem, mem_norm, w_k, w_v, k_gain, dims):
    B, M, D = mem.shape
    hm = _rmsnorm(mem.reshape(B * M, D), mem_norm)
    (k,) = _proj(hm, [w_k], functools.partial(_epi_group_rms, dims["dmq"]), [F32], cols=[k_gain], name="mem_k")
    (v,) = _proj(hm, [w_v], _epi_plain, [F32], name="mem_v")
    MH = dims["MH"]
    return k.reshape(B, M, MH, -1), v.reshape(B, M, MH, -1)


def kernel(x_prompt, x_sample, cache_attn_k, cache_attn_v, cache_conv, cache_mem_k, cache_mem_v, mem_prompt, rel_bias_table, ffn1_norm, ffn1_w_gate, ffn1_w_up, ffn1_w_down, mix_norm, w_in, attn_q_norm, attn_k_norm, lambda_q1, lambda_k1, lambda_q2, lambda_k2, attn_subln, w_attn_out, conv_dw, conv_dw_bias, conv_ln_g, conv_ln_b, w_conv_out, mem_norm, w_mem_kv, mem_q_norm, mem_k_norm, w_mem_out, w_out, ffn2_norm, ffn2_w_gate, ffn2_w_up, ffn2_w_down):
    depth = w_in.shape[0]
    D = x_prompt.shape[-1]
    H = cache_attn_k.shape[3]
    d = attn_q_norm.shape[-1]
    C = conv_dw.shape[-1]
    MH = cache_mem_k.shape[3]
    dmq = cache_mem_k.shape[4]
    dims = {"H": H, "d": d, "MH": MH, "dmq": dmq}
    c_q = H * 2 * d
    c_k, c_v = 2 * c_q, 3 * c_q
    c_conv = c_v + 2 * C
    c_memq = c_conv + MH * dmq
    bf = lambda a: a.astype(BF16)

    yp, ys = x_prompt, x_sample
    outs = [[] for _ in range(8)]
    for l in range(depth):
        lam_init = 0.8 - 0.6 * math.exp(-0.3 * l)
        wi = w_in[l]
        p = {
            "ffn1_norm": ffn1_norm[l], "ffn1_w_gate": bf(ffn1_w_gate[l]), "ffn1_w_up": bf(ffn1_w_up[l]),
            "ffn1_w_down": bf(ffn1_w_down[l]), "mix_norm": mix_norm[l],
            "w_q": bf(wi[:, :c_q]), "w_k": bf(wi[:, c_q:c_k]), "w_v": bf(wi[:, c_k:c_v]),
            "w_glu_a": bf(wi[:, c_v:c_v + C]), "w_glu_b": bf(wi[:, c_v + C:c_conv]),
            "w_mq": bf(wi[:, c_conv:c_memq]), "w_gates": bf(wi[:, c_memq:]),
            "q_gain": jnp.tile(attn_q_norm[l] * (d ** -0.5 * LOG2E), 2 * H).reshape(1, c_q),
            "k_gain": jnp.tile(attn_k_norm[l], 2 * H).reshape(1, c_q),
            "mq_gain": jnp.tile(mem_q_norm[l] * (dmq ** -0.5 * LOG2E), MH).reshape(1, MH * dmq),
            "lam_vecs": jnp.stack([lambda_q1[l], lambda_k1[l], lambda_q2[l], lambda_k2[l]]).astype(F32),
            "attn_subln": attn_subln[l], "w_attn_out": bf(w_attn_out[l]),
            "conv_dw": conv_dw[l], "conv_dw_bias": conv_dw_bias[l],
            "conv_ln_g": conv_ln_g[l], "conv_ln_b": conv_ln_b[l], "w_conv_out": bf(w_conv_out[l]),
            "w_mem_out": bf(w_mem_out[l]), "w_out": bf(w_out[l]),
            "ffn2_norm": ffn2_norm[l], "ffn2_w_gate": bf(ffn2_w_gate[l]), "ffn2_w_up": bf(ffn2_w_up[l]),
            "ffn2_w_down": bf(ffn2_w_down[l]),
        }
        wkv = w_mem_kv[l]
        mk, mv = _memory_kv(mem_prompt, mem_norm[l], bf(wkv[:, :MH * dmq]), bf(wkv[:, MH * dmq:]),
                            jnp.tile(mem_k_norm[l], MH).reshape(1, MH * dmq), dims)
        yp, kp, vp, cp = _layer_group(yp, None, mk, mv, rel_bias_table, lam_init, p, dims)
        ys, ks, vs, cs = _layer_group(ys, (cache_attn_k[l], cache_attn_v[l], cache_conv[l]),
                                      cache_mem_k[l], cache_mem_v[l], rel_bias_table, lam_init, p, dims)
        for lst, val in zip(outs, (kp, vp, cp, mk, mv, ks, vs, cs)):
            lst.append(val)
    return (yp, ys) + tuple(jnp.stack(o) for o in outs)
```

```python
import functools
import math

import numpy as np
import jax
import jax.numpy as jnp
from jax import lax
from jax.experimental import pallas as pl
from jax.experimental.pallas import tpu as pltpu

F32 = jnp.float32
BF16 = jnp.bfloat16

EPS = 1e-6
NEG_INF = -1e30
CHUNK = 64
REL_BUCKETS = 32
REL_MAX_DIST = 128
LOG2E = math.log2(math.e)

V7X_VMEM_LIMIT_MAX = 56 * 1024 * 1024
V7X_LANES = 128
V7X_MXU_DIM = 256
CONV_HALO = 32


def _cparams(n_axes, vmem_bytes):
    limit = min(max(int(vmem_bytes * 1.2) + (6 << 20), 32 << 20), V7X_VMEM_LIMIT_MAX)
    return pltpu.CompilerParams(dimension_semantics=("arbitrary",) * n_axes,
                                vmem_limit_bytes=limit)


def _tile(n, pref):
    if n <= pref:
        return n
    t = pref
    while n % t:
        t //= 2
    return t


def _rms(x, g):
    return x * lax.rsqrt(jnp.mean(x * x, axis=-1, keepdims=True) + EPS) * g


def _group_rms(x, g, width):
    parts = []
    for c in range(0, x.shape[-1], width):
        parts.append(_rms(x[:, c:c + width], g[:, c:c + width]))
    return parts[0] if len(parts) == 1 else jnp.concatenate(parts, axis=-1)


def _rmsnorm_body(x_ref, g_ref, o_ref):
    o_ref[...] = _rms(x_ref[...], g_ref[...]).astype(o_ref.dtype)


def _rmsnorm(x, g):
    R, D = x.shape
    tm = _tile(R, 256)
    return pl.pallas_call(
        _rmsnorm_body,
        grid=(R // tm,),
        in_specs=[pl.BlockSpec((tm, D), lambda i: (i, 0)),
                  pl.BlockSpec((1, D), lambda i: (0, 0))],
        out_specs=pl.BlockSpec((tm, D), lambda i: (i, 0)),
        out_shape=jax.ShapeDtypeStruct((R, D), BF16),
        compiler_params=_cparams(1, 2 * tm * D * 6),
        name="rmsnorm",
    )(x, g.reshape(1, D))


def _ffn_body(emit_norm, nj, x_ref, g_ref, wg_ref, wu_ref, wd_ref, *rest):
    if emit_norm:
        g2_ref, out_ref, hn_ref, xn_scr = rest
    else:
        out_ref, xn_scr = rest
    j = pl.program_id(1)

    tm = out_ref.shape[0]
    rc = min(tm, 64)

    @pl.when(j == 0)
    def _():
        for r0 in range(0, tm, rc):
            x = x_ref[r0:r0 + rc, :]
            xn_scr[r0:r0 + rc, :] = _rms(x, g_ref[...]).astype(BF16)
            out_ref[r0:r0 + rc, :] = x

    xn = xn_scr[...]
    a = jnp.dot(xn, wg_ref[...], preferred_element_type=F32)
    b = jnp.dot(xn, wu_ref[...], preferred_element_type=F32)
    h = (a * jax.nn.sigmoid(a) * b * 0.5).astype(BF16)
    D = out_ref.shape[1]
    cw = min(D, 1024)
    for c0 in range(0, D, cw):
        out_ref[:, c0:c0 + cw] += jnp.dot(h, wd_ref[:, c0:c0 + cw], preferred_element_type=F32)

    if emit_norm:
        @pl.when(j == nj - 1)
        def _():
            for r0 in range(0, tm, rc):
                hn_ref[r0:r0 + rc, :] = _rms(out_ref[r0:r0 + rc, :], g2_ref[...]).astype(BF16)


def _ffn(x, g, wg, wu, wd, g2=None):
    R, D = x.shape
    Fdim = wg.shape[1]
    tm = _tile(R, 512)
    tf = _tile(Fdim, V7X_MXU_DIM)
    nj = Fdim // tf
    emit_norm = g2 is not None
    in_specs = [pl.BlockSpec((tm, D), lambda i, j: (i, 0), pipeline_mode=pl.Buffered(1)),
                pl.BlockSpec((1, D), lambda i, j: (0, 0)),
                pl.BlockSpec((D, tf), lambda i, j: (0, j)),
                pl.BlockSpec((D, tf), lambda i, j: (0, j)),
                pl.BlockSpec((tf, D), lambda i, j: (j, 0))]
    args = [x, g.reshape(1, D), wg, wu, wd]
    out_specs = [pl.BlockSpec((tm, D), lambda i, j: (i, 0))]
    out_shape = [jax.ShapeDtypeStruct((R, D), F32)]
    vmem = tm * D * 4 + 2 * tm * D * 4 + tm * D * 2 + 2 * 3 * D * tf * 2 + 4 * tm * tf * 4
    if emit_norm:
        in_specs.append(pl.BlockSpec((1, D), lambda i, j: (0, 0)))
        args.append(g2.reshape(1, D))
        out_specs.append(pl.BlockSpec((tm, D), lambda i, j: (i, 0)))
        out_shape.append(jax.ShapeDtypeStruct((R, D), BF16))
        vmem += 2 * tm * D * 2
    res = pl.pallas_call(
        functools.partial(_ffn_body, emit_norm, nj),
        grid=(R // tm, nj),
        in_specs=in_specs,
        out_specs=out_specs,
        out_shape=out_shape,
        scratch_shapes=[pltpu.VMEM((tm, D), BF16)],
        compiler_params=_cparams(2, vmem),
        name="ffn_norm" if emit_norm else "ffn",
    )(*args)
    return res if emit_norm else res[0]


def _proj_body(epi, n_w, n_ex, a_ref, *refs):
    w_refs = refs[:n_w]
    ex_refs = refs[n_w:n_w + n_ex]
    out_refs = refs[n_w + n_ex:]
    a = a_ref[...]
    accs = [jnp.dot(a, w[...], preferred_element_type=F32) for w in w_refs]
    y = epi(accs, [e[...] for e in ex_refs])
    for o in out_refs:
        o[...] = y.astype(o.dtype)


def _proj(a, ws, epi, out_dtypes, *, cols=(), tiles=(), name, tm_pref=1024, tn_pref=512):
    R, K = a.shape
    N = ws[0].shape[1]
    tm = _tile(R, tm_pref)
    tn = _tile(N, tn_pref)
    in_specs = [pl.BlockSpec((tm, K), lambda i, j: (i, 0))]
    in_specs += [pl.BlockSpec((K, tn), lambda i, j: (0, j)) for _ in ws]
    in_specs += [pl.BlockSpec((1, tn), lambda i, j: (0, j)) for _ in cols]
    in_specs += [pl.BlockSpec((tm, tn), lambda i, j: (i, j)) for _ in tiles]
    vmem = 2 * tm * K * 2 + 2 * len(ws) * K * tn * 2 + (len(ws) + 2) * tm * tn * 4
    vmem += sum(2 * tm * tn * t.dtype.itemsize for t in tiles)
    vmem += sum(2 * tm * tn * jnp.dtype(d).itemsize for d in out_dtypes)
    res = pl.pallas_call(
        functools.partial(_proj_body, epi, len(ws), len(cols) + len(tiles)),
        grid=(R // tm, N // tn),
        in_specs=in_specs,
        out_specs=[pl.BlockSpec((tm, tn), lambda i, j: (i, j)) for _ in out_dtypes],
        out_shape=[jax.ShapeDtypeStruct((R, N), d) for d in out_dtypes],
        compiler_params=_cparams(2, vmem),
        name=name,
    )(a, *ws, *cols, *tiles)
    return res


def _epi_plain(accs, ex):
    return accs[0]


def _epi_group_rms(width, accs, ex):
    return _group_rms(accs[0], ex[0], width)


def _epi_glu(accs, ex):
    return accs[0] * jax.nn.sigmoid(accs[1])


def _epi_sigmoid(accs, ex):
    return jax.nn.sigmoid(accs[0])


def _epi_residual(accs, ex):
    return ex[0] + accs[0]


def _merge_body(a_ref, c_ref, m_ref, wa_ref, wc_ref, wm_ref, ga_ref, gc_ref, gm_ref, o_ref):
    ya = jnp.dot(a_ref[...], wa_ref[...], preferred_element_type=F32)
    yc = jnp.dot(c_ref[...], wc_ref[...], preferred_element_type=F32)
    ym = jnp.dot(m_ref[...], wm_ref[...], preferred_element_type=F32)
    merged = ga_ref[...].astype(F32) * ya + gc_ref[...].astype(F32) * yc + gm_ref[...].astype(F32) * ym
    o_ref[...] = merged.astype(o_ref.dtype)


def _merge(oa, oc, om, wa, wc, wm, gates):
    R = oa.shape[0]
    D = wa.shape[1]
    tm = _tile(R, 512)
    tn = _tile(D, 512)
    nt = D // tn
    row = lambda arr: pl.BlockSpec((tm, arr.shape[1]), lambda i, j: (i, 0))
    col = lambda arr: pl.BlockSpec((arr.shape[0], tn), lambda i, j: (0, j))
    gate = lambda b: pl.BlockSpec((tm, tn), lambda i, j: (i, b * nt + j))
    ksum = oa.shape[1] + oc.shape[1] + om.shape[1]
    vmem = 2 * tm * ksum * 2 + 2 * ksum * tn * 2 + 2 * 4 * tm * tn * 2 + 4 * tm * tn * 4
    return pl.pallas_call(
        _merge_body,
        grid=(R // tm, nt),
        in_specs=[row(oa), row(oc), row(om), col(wa), col(wc), col(wm), gate(0), gate(1), gate(2)],
        out_specs=pl.BlockSpec((tm, tn), lambda i, j: (i, j)),
        out_shape=jax.ShapeDtypeStruct((R, D), BF16),
        compiler_params=_cparams(2, vmem),
        name="merge",
    )(oa, oc, om, wa, wc, wm, gates, gates, gates)


def _t5_bucket(rel):
    nb = REL_BUCKETS // 2
    max_exact = nb // 2
    n = jnp.abs(rel)
    nf = jnp.maximum(n, max_exact).astype(F32)
    large = max_exact + (jnp.log(nf / max_exact) / math.log(REL_MAX_DIST / max_exact)
                         * (nb - max_exact)).astype(jnp.int32)
    large = jnp.minimum(large, nb - 1)
    return jnp.where(rel > 0, nb, 0) + jnp.where(n < max_exact, n, large)


def _bias_window(table, q0, nq, k0, nk):
    span = nq + nk - 1
    rel = (k0 - q0) - (nq - 1) + np.arange(span, dtype=np.int32)
    far = table[REL_BUCKETS // 2 - 1].astype(F32)
    v = ((table[_t5_bucket(jnp.asarray(rel))].astype(F32) - far) * LOG2E).T
    h = v.shape[0]
    rows = jnp.tile(jnp.pad(v, ((0, 0), (0, 1))), (1, nq))[:, :nq * span].reshape(h, nq, span)
    b = rows[:, :, nq - 1:nq - 1 + nk]
    qpos = q0 + np.arange(nq)
    kpos = k0 + np.arange(nk)
    visible = (kpos[None, :] // CHUNK) <= (qpos[:, None] // CHUNK)
    return jnp.where(jnp.asarray(visible)[None], b, NEG_INF)


def _lambda(lam_ref, lam_init):
    lv = lam_ref[...]
    s1 = jnp.sum(lv[0:1] * lv[1:2], axis=-1, keepdims=True)
    s2 = jnp.sum(lv[2:3] * lv[3:4], axis=-1, keepdims=True)
    return jnp.exp(s1) - jnp.exp(s2) + lam_init


def _attn_prompt_body(t, d, lam_init, qt_ref, k_ref, vt_ref, bias_ref, lam_ref, g_ref, o_ref,
                      s_scr, m_scr, l_scr, acc_scr):
    i = pl.program_id(1)
    m_scr[...] = jnp.full(m_scr.shape, NEG_INF, F32)
    l_scr[...] = jnp.zeros(l_scr.shape, F32)
    acc_scr[...] = jnp.zeros(acc_scr.shape, F32)

    def scores(j):
        bias = bias_ref[0, jnp.clip(j - i + 2, 0, 2)]
        row0 = pl.multiple_of(j * t, t)
        return [jnp.dot(k_ref[pl.ds(row0, t), s * d:(s + 1) * d], qt_ref[0, 0, s * d:(s + 1) * d, :],
                        preferred_element_type=F32) + bias for s in range(2)]

    def softmax_pv(j):
        vt = vt_ref[0, j]
        for s in range(2):
            sc = s_scr[s]
            m_old = m_scr[s]
            m_new = jnp.maximum(m_old, jnp.max(sc, axis=0, keepdims=True))
            alpha = jnp.exp2(m_old - m_new)
            p = jnp.exp2(sc - m_new)
            l_scr[s] = alpha * l_scr[s] + jnp.sum(p, axis=0, keepdims=True)
            acc_scr[s] = alpha * acc_scr[s] + jnp.dot(vt, p.astype(BF16), preferred_element_type=F32)
            m_scr[s] = m_new

    def put(sc):
        for s in range(2):
            s_scr[s] = sc[s]

    put(scores(0))

    def body(j, carry):
        nxt = scores(j + 1)
        softmax_pv(j)
        put(nxt)
        return carry

    lax.fori_loop(0, i, body, 0)
    softmax_pv(i)

    lam = _lambda(lam_ref, lam_init)
    o = acc_scr[0] / l_scr[0] - lam * (acc_scr[1] / l_scr[1])
    o = o * lax.rsqrt(jnp.mean(o * o, axis=0, keepdims=True) + EPS) * g_ref[...] * (1.0 - lam_init)
    o_ref[...] = o.T.astype(o_ref.dtype)


def _attn_prompt(q, k, v, table, lam_vecs, subln, lam_init, n_heads, t_pref=512):
    T = q.shape[0]
    w = q.shape[1] // n_heads
    d = w // 2
    t = _tile(T, t_pref)
    nt = T // t
    assert t % CHUNK == 0 and t >= REL_MAX_DIST, (t, "key tiles two back must be at constant bias")
    qt = q.reshape(nt, t, n_heads, w).transpose(2, 0, 3, 1)
    vt = v.reshape(nt, t, n_heads, w).transpose(2, 0, 3, 1)
    diag = _bias_window(table, 0, t, 0, t)
    prev = _bias_window(table, t, t, 0, t)
    bias = jnp.stack([jnp.zeros_like(diag), prev, diag], axis=1).transpose(0, 1, 3, 2)
    vmem = 2 * T * w * 2 + 2 * 3 * t * t * 4 + 4 * t * w * 2 + 2 * t * w * 4 + 8 * t * t * 4
    return pl.pallas_call(
        functools.partial(_attn_prompt_body, t, d, lam_init),
        grid=(n_heads, nt),
        in_specs=[pl.BlockSpec((1, 1, w, t), lambda h, i: (h, i, 0, 0)),
                  pl.BlockSpec((T, w), lambda h, i: (0, h), pipeline_mode=pl.Buffered(1)),
                  pl.BlockSpec((1, nt, w, t), lambda h, i: (h, 0, 0, 0), pipeline_mode=pl.Buffered(1)),
                  pl.BlockSpec((1, 3, t, t), lambda h, i: (h, 0, 0, 0)),
                  pl.BlockSpec((4, d), lambda h, i: (0, 0)),
                  pl.BlockSpec((w, 1), lambda h, i: (0, 0))],
        out_specs=pl.BlockSpec((t, w), lambda h, i: (i, h)),
        out_shape=jax.ShapeDtypeStruct((T, n_heads * w), BF16),
        scratch_shapes=[pltpu.VMEM((2, t, t), F32), pltpu.VMEM((2, 1, t), F32), pltpu.VMEM((2, 1, t), F32),
                        pltpu.VMEM((2, w, t), F32)],
        compiler_params=_cparams(2, vmem),
        name="attn_prompt",
    )(qt, k, vt, bias, lam_vecs, subln.reshape(w, 1))


def _attn_sample_body(d, lam_init, q_ref, ck_ref, cv_ref, kn_ref, vn_ref, mb_ref, mbn_ref,
                      lam_ref, g_ref, o_ref, m_scr, l_scr, acc_scr):
    pj = pl.program_id(1)
    last = pl.num_programs(1) - 1
    nt = (((1,), (1,)), ((), ()))

    @pl.when(pj == 0)
    def _():
        m_scr[...] = jnp.full(m_scr.shape, NEG_INF, F32)
        l_scr[...] = jnp.zeros(l_scr.shape, F32)
        acc_scr[...] = jnp.zeros(acc_scr.shape, F32)

    def update(kb, vb, mb):
        for s in range(2):
            g = lax.dot_general(q_ref[0, s], kb[:, s * d:(s + 1) * d], nt, preferred_element_type=F32) + mb
            m_old = m_scr[s]
            m_new = jnp.maximum(m_old, jnp.max(g, axis=-1, keepdims=True))
            alpha = jnp.exp2(m_old - m_new)
            p = jnp.exp2(g - m_new)
            l_scr[s] = alpha * l_scr[s] + jnp.sum(p, axis=-1, keepdims=True)
            acc_scr[s] = alpha * acc_scr[s] + jnp.dot(p.astype(BF16), vb, preferred_element_type=F32)
            m_scr[s] = m_new

    slot = (pj == last).astype(jnp.int32)
    rows = ck_ref.shape[1]
    ch = min(rows, 1024)
    for r0 in range(0, rows, ch):
        update(ck_ref[0, r0:r0 + ch, :].astype(BF16), cv_ref[0, r0:r0 + ch, :].astype(BF16),
               mb_ref[slot, :, r0:r0 + ch])

    @pl.when(pj == last)
    def _():
        update(kn_ref[0], vn_ref[0], mbn_ref[...])
        lam = _lambda(lam_ref, lam_init)
        o = acc_scr[0] / l_scr[0] - lam * (acc_scr[1] / l_scr[1])
        o_ref[0] = (_rms(o, g_ref[...]) * (1.0 - lam_init)).astype(o_ref.dtype)


def _head_mask_bias(bias, n_heads):
    same = jnp.eye(n_heads, dtype=bool)[:, None, None, :]
    full = jnp.where(same, bias[..., None], NEG_INF)
    return full.reshape(bias.shape[0] * bias.shape[1], bias.shape[2] * n_heads)


def _attn_sample(q, kn, vn, cache_k, cache_v, table, lam_vecs, subln, lam_init, n_heads):
    B, S, HW = q.shape
    P = cache_k.shape[1]
    w = HW // n_heads
    d = w // 2
    tp = _tile(P, 256)
    hs = n_heads * S
    assert tp >= REL_MAX_DIST or tp == P, "only the newest cache tile may carry a non-constant bias"
    near = _bias_window(table, P, S, P - tp, tp)
    mb = jnp.stack([_head_mask_bias(jnp.zeros_like(near), n_heads), _head_mask_bias(near, n_heads)])
    mbn = _head_mask_bias(_bias_window(table, P, S, P, S), n_heads)
    qs = q.reshape(B, S, n_heads, 2, d).transpose(0, 3, 2, 1, 4).reshape(B, 2, hs, d)
    new = pl.BlockSpec((1, S * n_heads, w), lambda b, j: (b, 0, 0))
    old = pl.BlockSpec((1, tp * n_heads, w), lambda b, j: (b, j, 0))
    vmem = (2 * 2 * tp * HW * 4 + 2 * tp * HW * 2 + 2 * hs * tp * n_heads * 4 + 4 * hs * tp * n_heads * 4
            + 4 * hs * w * 4)
    o = pl.pallas_call(
        functools.partial(_attn_sample_body, d, lam_init),
        grid=(B, P // tp),
        in_specs=[pl.BlockSpec((1, 2, hs, d), lambda b, j: (b, 0, 0, 0)),
                  old, old, new, new,
                  pl.BlockSpec((2, hs, tp * n_heads), lambda b, j: (0, 0, 0), pipeline_mode=pl.Buffered(1)),
                  pl.BlockSpec((hs, S * n_heads), lambda b, j: (0, 0)),
                  pl.BlockSpec((4, d), lambda b, j: (0, 0)),
                  pl.BlockSpec((1, w), lambda b, j: (0, 0))],
        out_specs=pl.BlockSpec((1, hs, w), lambda b, j: (b, 0, 0)),
        out_shape=jax.ShapeDtypeStruct((B, hs, w), BF16),
        scratch_shapes=[pltpu.VMEM((2, hs, 1), F32), pltpu.VMEM((2, hs, 1), F32), pltpu.VMEM((2, hs, w), F32)],
        compiler_params=_cparams(2, vmem),
        name="attn_sample",
    )(qs, cache_k.reshape(B, P * n_heads, w), cache_v.reshape(B, P * n_heads, w),
      kn.reshape(B, S * n_heads, w), vn.reshape(B, S * n_heads, w), mb, mbn, lam_vecs, subln.reshape(1, w))
    return o.reshape(B, n_heads, S, w).transpose(0, 2, 1, 3).reshape(B, S, HW)


def _conv_body(width, halo_ref, main_ref, w_ref, b_ref, g_ref, beta_ref, o_ref, win_scr, conv_scr):
    tm = main_ref.shape[1]
    C = main_ref.shape[2]
    win_scr[0:CONV_HALO, :] = halo_ref[0]
    win_scr[CONV_HALO:CONV_HALO + tm, :] = main_ref[0]
    rc = min(tm, 32)
    cc = min(C, 512)
    first = CONV_HALO - (width - 1)
    for r0 in range(0, tm, rc):
        for c0 in range(0, C, cc):
            acc = jnp.broadcast_to(b_ref[:, c0:c0 + cc], (rc, cc))
            for tap in range(width):
                acc = acc + win_scr[r0 + first + tap:r0 + first + tap + rc, c0:c0 + cc] * w_ref[tap:tap + 1, c0:c0 + cc]
            conv_scr[r0:r0 + rc, c0:c0 + cc] = acc
    c = conv_scr[...]
    xc = c - jnp.mean(c, axis=-1, keepdims=True)
    y = xc * lax.rsqrt(jnp.mean(xc * xc, axis=-1, keepdims=True) + EPS) * g_ref[...] + beta_ref[...]
    o_ref[0] = (y * jax.nn.sigmoid(y)).astype(o_ref.dtype)


def _conv_module(upad, pad, w, b, g, beta):
    B, TP, C = upad.shape
    T = TP - pad
    width = w.shape[0]
    tm = _tile(T, 64)
    assert pad % tm == 0 and pad >= CONV_HALO and (tm % CONV_HALO == 0 or tm == T)
    vec = lambda: pl.BlockSpec((1, C), lambda bb, i: (0, 0))
    return pl.pallas_call(
        functools.partial(_conv_body, width),
        grid=(B, T // tm),
        in_specs=[pl.BlockSpec((1, CONV_HALO, C), lambda bb, i: (bb, (pad + i * tm) // CONV_HALO - 1, 0)),
                  pl.BlockSpec((1, tm, C), lambda bb, i: (bb, pad // tm + i, 0)),
                  pl.BlockSpec((width, C), lambda bb, i: (0, 0)),
                  vec(), vec(), vec()],
        out_specs=pl.BlockSpec((1, tm, C), lambda bb, i: (bb, i, 0)),
        out_shape=jax.ShapeDtypeStruct((B, T, C), BF16),
        scratch_shapes=[pltpu.VMEM((CONV_HALO + tm, C), F32), pltpu.VMEM((tm, C), F32)],
        compiler_params=_cparams(2, 16 * (CONV_HALO + tm) * C),
        name="conv_module",
    )(upad, upad, w, b.reshape(1, C), g.reshape(1, C), beta.reshape(1, C))


def _mem_attn_body(n_heads, dq, dv, q_ref, k_ref, v_ref, o_ref):
    nt = (((1,), (1,)), ((), ()))
    outs = []
    for h in range(n_heads):
        qh = q_ref[0, :, h * dq:(h + 1) * dq]
        kh = k_ref[0, :, h * dq:(h + 1) * dq].astype(BF16)
        vh = v_ref[0, :, h * dv:(h + 1) * dv].astype(BF16)
        s = lax.dot_general(qh, kh, nt, preferred_element_type=F32)
        p = jnp.exp2(s - jnp.max(s, axis=-1, keepdims=True))
        l = jnp.sum(p, axis=-1, keepdims=True)
        outs.append(jnp.dot(p.astype(BF16), vh, preferred_element_type=F32) / l)
    o_ref[0] = jnp.concatenate(outs, axis=-1).astype(o_ref.dtype)


def _mem_attn(mq, mem_k, mem_v, n_heads):
    B, T, QW = mq.shape
    M = mem_k.shape[1]
    VW = mem_v.shape[2]
    tm = _tile(T, 512)
    return pl.pallas_call(
        functools.partial(_mem_attn_body, n_heads, QW // n_heads, VW // n_heads),
        grid=(B, T // tm),
        in_specs=[pl.BlockSpec((1, tm, QW), lambda b, i: (b, i, 0)),
                  pl.BlockSpec((1, M, QW), lambda b, i: (b, 0, 0)),
                  pl.BlockSpec((1, M, VW), lambda b, i: (b, 0, 0))],
        out_specs=pl.BlockSpec((1, tm, VW), lambda b, i: (b, i, 0)),
        out_shape=jax.ShapeDtypeStruct((B, T, VW), BF16),
        compiler_params=_cparams(2, 2 * (tm * QW * 2 + M * (QW + VW) * 4 + tm * VW * 2) + 3 * tm * VW * 4),
        name="mem_attn",
    )(mq, mem_k, mem_v)


def _layer_group(x, past, mem_k, mem_v, table, lam_init, p, dims):
    B, T, D = x.shape
    H, d, MH = dims["H"], dims["d"], dims["MH"]
    R = B * T
    xf = x.reshape(R, D)

    x1, h = _ffn(xf, p["ffn1_norm"], p["ffn1_w_gate"], p["ffn1_w_up"], p["ffn1_w_down"], p["mix_norm"])

    (q,) = _proj(h, [p["w_q"]], functools.partial(_epi_group_rms, d), [BF16], cols=[p["q_gain"]], name="proj_q")
    k32, k16 = _proj(h, [p["w_k"]], functools.partial(_epi_group_rms, d), [F32, BF16], cols=[p["k_gain"]],
                     name="proj_k")
    v32, v16 = _proj(h, [p["w_v"]], _epi_plain, [F32, BF16], name="proj_v")
    (u,) = _proj(h, [p["w_glu_a"], p["w_glu_b"]], _epi_glu, [F32], name="proj_glu")
    (mq,) = _proj(h, [p["w_mq"]], functools.partial(_epi_group_rms, dims["dmq"]), [BF16], cols=[p["mq_gain"]],
                  name="proj_mq")
    (gates,) = _proj(h, [p["w_gates"]], _epi_sigmoid, [BF16], name="proj_gates")

    HW = H * 2 * d
    C = u.shape[1]
    width = p["conv_dw"].shape[0]
    if past is None:
        assert B == 1
        oa = _attn_prompt(q, k16, v16, table, p["lam_vecs"], p["attn_subln"], lam_init, H)
        hist = jnp.zeros((B, width - 1, C), F32)
    else:
        cache_k, cache_v, hist = past
        oa = _attn_sample(q.reshape(B, T, HW), k16.reshape(B, T, HW), v16.reshape(B, T, HW), cache_k, cache_v,
                          table, p["lam_vecs"], p["attn_subln"], lam_init, H).reshape(R, HW)

    tmc = _tile(T, 64)
    pad = max(tmc, CONV_HALO)
    upad = jnp.concatenate([jnp.zeros((B, pad - (width - 1), C), F32), hist, u.reshape(B, T, C)], axis=1)
    conv_state = upad[:, pad + T - (width - 1):]
    oc = _conv_module(upad, pad, p["conv_dw"], p["conv_dw_bias"], p["conv_ln_g"], p["conv_ln_b"]).reshape(R, C)

    M = mem_k.shape[1]
    om = _mem_attn(mq.reshape(B, T, -1), mem_k.reshape(B, M, -1), mem_v.reshape(B, M, -1), MH)
    om = om.reshape(R, -1)

    merged = _merge(oa, oc, om, p["w_attn_out"], p["w_conv_out"], p["w_mem_out"], gates)
    (x2,) = _proj(merged, [p["w_out"]], _epi_residual, [F32], tiles=[x1], name="proj_out")
    y = _ffn(x2, p["ffn2_norm"], p["ffn2_w_gate"], p["ffn2_w_up"], p["ffn2_w_down"])
    return (y.reshape(B, T, D), k32.reshape(B, T, H, 2 * d), v32.reshape(B, T, H, 2 * d), conv_state)


def _memory_kv(mem, mem_norm, w_k, w_v, k_gain, dims):
    B, M, D = mem.shape
    hm = _rmsnorm(mem.reshape(B * M, D), mem_norm)
    (k,) = _proj(hm, [w_k], functools.partial(_epi_group_rms, dims["dmq"]), [F32], cols=[k_gain], name="mem_k")
    (v,) = _proj(hm, [w_v], _epi_plain, [F32], name="mem_v")
    MH = dims["MH"]
    return k.reshape(B, M, MH, -1), v.reshape(B, M, MH, -1)


def kernel(x_prompt, x_sample, cache_attn_k, cache_attn_v, cache_conv, cache_mem_k, cache_mem_v, mem_prompt, rel_bias_table, ffn1_norm, ffn1_w_gate, ffn1_w_up, ffn1_w_down, mix_norm, w_in, attn_q_norm, attn_k_norm, lambda_q1, lambda_k1, lambda_q2, lambda_k2, attn_subln, w_attn_out, conv_dw, conv_dw_bias, conv_ln_g, conv_ln_b, w_conv_out, mem_norm, w_mem_kv, mem_q_norm, mem_k_norm, w_mem_out, w_out, ffn2_norm, ffn2_w_gate, ffn2_w_up, ffn2_w_down):
    depth = w_in.shape[0]
    D = x_prompt.shape[-1]
    H = cache_attn_k.shape[3]
    d = attn_q_norm.shape[-1]
    C = conv_dw.shape[-1]
    MH = cache_mem_k.shape[3]
    dmq = cache_mem_k.shape[4]
    dims = {"H": H, "d": d, "MH": MH, "dmq": dmq}
    c_q = H * 2 * d
    c_k, c_v = 2 * c_q, 3 * c_q
    c_conv = c_v + 2 * C
    c_memq = c_conv + MH * dmq
    bf = lambda a: a.astype(BF16)

    yp, ys = x_prompt, x_sample
    outs = [[] for _ in range(8)]
    for l in range(depth):
        lam_init = 0.8 - 0.6 * math.exp(-0.3 * l)
        wi = w_in[l]
        p = {
            "ffn1_norm": ffn1_norm[l], "ffn1_w_gate": bf(ffn1_w_gate[l]), "ffn1_w_up": bf(ffn1_w_up[l]),
            "ffn1_w_down": bf(ffn1_w_down[l]), "mix_norm": mix_norm[l],
            "w_q": bf(wi[:, :c_q]), "w_k": bf(wi[:, c_q:c_k]), "w_v": bf(wi[:, c_k:c_v]),
            "w_glu_a": bf(wi[:, c_v:c_v + C]), "w_glu_b": bf(wi[:, c_v + C:c_conv]),
            "w_mq": bf(wi[:, c_conv:c_memq]), "w_gates": bf(wi[:, c_memq:]),
            "q_gain": jnp.tile(attn_q_norm[l] * (d ** -0.5 * LOG2E), 2 * H).reshape(1, c_q),
            "k_gain": jnp.tile(attn_k_norm[l], 2 * H).reshape(1, c_q),
            "mq_gain": jnp.tile(mem_q_norm[l] * (dmq ** -0.5 * LOG2E), MH).reshape(1, MH * dmq),
            "lam_vecs": jnp.stack([lambda_q1[l], lambda_k1[l], lambda_q2[l], lambda_k2[l]]).astype(F32),
            "attn_subln": attn_subln[l], "w_attn_out": bf(w_attn_out[l]),
            "conv_dw": conv_dw[l], "conv_dw_bias": conv_dw_bias[l],
            "conv_ln_g": conv_ln_g[l], "conv_ln_b": conv_ln_b[l], "w_conv_out": bf(w_conv_out[l]),
            "w_mem_out": bf(w_mem_out[l]), "w_out": bf(w_out[l]),
            "ffn2_norm": ffn2_norm[l], "ffn2_w_gate": bf(ffn2_w_gate[l]), "ffn2_w_up": bf(ffn2_w_up[l]),
            "ffn2_w_down": bf(ffn2_w_down[l]),
        }
        wkv = w_mem_kv[l]
        mk, mv = _memory_kv(mem_prompt, mem_norm[l], bf(wkv[:, :MH * dmq]), bf(wkv[:, MH * dmq:]),
                            jnp.tile(mem_k_norm[l], MH).reshape(1, MH * dmq), dims)
        yp, kp, vp, cp = _layer_group(yp, None, mk, mv, rel_bias_table, lam_init, p, dims)
        ys, ks, vs, cs = _layer_group(ys, (cache_attn_k[l], cache_attn_v[l], cache_conv[l]),
                                      cache_mem_k[l], cache_mem_v[l], rel_bias_table, lam_init, p, dims)
        for lst, val in zip(outs, (kp, vp, cp, mk, mv, ks, vs, cs)):
            lst.append(val)
    return (yp, ys) + tuple(jnp.stack(o) for o in outs)
```

```python
import functools
import math

import numpy as np
import jax
import jax.numpy as jnp
from jax import lax
from jax.experimental import pallas as pl
from jax.experimental.pallas import tpu as pltpu

F32 = jnp.float32
BF16 = jnp.bfloat16

EPS = 1e-6
NEG_INF = -1e30
CHUNK = 64
REL_BUCKETS = 32
REL_MAX_DIST = 128
LOG2E = math.log2(math.e)

V7X_VMEM_LIMIT_MAX = 56 * 1024 * 1024
V7X_LANES = 128
V7X_MXU_DIM = 256
CONV_HALO = 32


def _cparams(n_axes, vmem_bytes):
    limit = min(max(int(vmem_bytes * 1.2) + (6 << 20), 32 << 20), V7X_VMEM_LIMIT_MAX)
    return pltpu.CompilerParams(dimension_semantics=("arbitrary",) * n_axes,
                                vmem_limit_bytes=limit)


def _tile(n, pref):
    if n <= pref:
        return n
    t = pref
    while n % t:
        t //= 2
    return t


def _rms(x, g):
    return x * lax.rsqrt(jnp.mean(x * x, axis=-1, keepdims=True) + EPS) * g


def _group_rms(x, g, width):
    parts = []
    for c in range(0, x.shape[-1], width):
        parts.append(_rms(x[:, c:c + width], g[:, c:c + width]))
    return parts[0] if len(parts) == 1 else jnp.concatenate(parts, axis=-1)


def _rmsnorm_body(x_ref, g_ref, o_ref):
    o_ref[...] = _rms(x_ref[...], g_ref[...]).astype(o_ref.dtype)


def _rmsnorm(x, g):
    R, D = x.shape
    tm = _tile(R, 256)
    return pl.pallas_call(
        _rmsnorm_body,
        grid=(R // tm,),
        in_specs=[pl.BlockSpec((tm, D), lambda i: (i, 0)),
                  pl.BlockSpec((1, D), lambda i: (0, 0))],
        out_specs=pl.BlockSpec((tm, D), lambda i: (i, 0)),
        out_shape=jax.ShapeDtypeStruct((R, D), BF16),
        compiler_params=_cparams(1, 2 * tm * D * 6),
        name="rmsnorm",
    )(x, g.reshape(1, D))


def _ffn_body(emit_norm, nj, x_ref, g_ref, wg_ref, wu_ref, wd_ref, *rest):
    if emit_norm:
        g2_ref, out_ref, hn_ref, xn_scr = rest
    else:
        out_ref, xn_scr = rest
    j = pl.program_id(1)

    tm = out_ref.shape[0]
    rc = min(tm, 64)

    @pl.when(j == 0)
    def _():
        for r0 in range(0, tm, rc):
            x = x_ref[r0:r0 + rc, :]
            xn_scr[r0:r0 + rc, :] = _rms(x, g_ref[...]).astype(BF16)
            out_ref[r0:r0 + rc, :] = x

    xn = xn_scr[...]
    a = jnp.dot(xn, wg_ref[...], preferred_element_type=F32)
    b = jnp.dot(xn, wu_ref[...], preferred_element_type=F32)
    h = (a * jax.nn.sigmoid(a) * b * 0.5).astype(BF16)
    D = out_ref.shape[1]
    cw = min(D, 1024)
    for c0 in range(0, D, cw):
        out_ref[:, c0:c0 + cw] += jnp.dot(h, wd_ref[:, c0:c0 + cw], preferred_element_type=F32)

    if emit_norm:
        @pl.when(j == nj - 1)
        def _():
            for r0 in range(0, tm, rc):
                hn_ref[r0:r0 + rc, :] = _rms(out_ref[r0:r0 + rc, :], g2_ref[...]).astype(BF16)


def _ffn(x, g, wg, wu, wd, g2=None):
    R, D = x.shape
    Fdim = wg.shape[1]
    tm = _tile(R, 512)
    tf = _tile(Fdim, V7X_MXU_DIM)
    nj = Fdim // tf
    emit_norm = g2 is not None
    in_specs = [pl.BlockSpec((tm, D), lambda i, j: (i, 0), pipeline_mode=pl.Buffered(1)),
                pl.BlockSpec((1, D), lambda i, j: (0, 0)),
                pl.BlockSpec((D, tf), lambda i, j: (0, j)),
                pl.BlockSpec((D, tf), lambda i, j: (0, j)),
                pl.BlockSpec((tf, D), lambda i, j: (j, 0))]
    args = [x, g.reshape(1, D), wg, wu, wd]
    out_specs = [pl.BlockSpec((tm, D), lambda i, j: (i, 0))]
    out_shape = [jax.ShapeDtypeStruct((R, D), F32)]
    vmem = tm * D * 4 + 2 * tm * D * 4 + tm * D * 2 + 2 * 3 * D * tf * 2 + 4 * tm * tf * 4
    if emit_norm:
        in_specs.append(pl.BlockSpec((1, D), lambda i, j: (0, 0)))
        args.append(g2.reshape(1, D))
        out_specs.append(pl.BlockSpec((tm, D), lambda i, j: (i, 0)))
        out_shape.append(jax.ShapeDtypeStruct((R, D), BF16))
        vmem += 2 * tm * D * 2
    res = pl.pallas_call(
        functools.partial(_ffn_body, emit_norm, nj),
        grid=(R // tm, nj),
        in_specs=in_specs,
        out_specs=out_specs,
        out_shape=out_shape,
        scratch_shapes=[pltpu.VMEM((tm, D), BF16)],
        compiler_params=_cparams(2, vmem),
        name="ffn_norm" if emit_norm else "ffn",
    )(*args)
    return res if emit_norm else res[0]


def _proj_body(epi, n_w, n_ex, a_ref, *refs):
    w_refs = refs[:n_w]
    ex_refs = refs[n_w:n_w + n_ex]
    out_refs = refs[n_w + n_ex:]
    a = a_ref[...]
    accs = [jnp.dot(a, w[...], preferred_element_type=F32) for w in w_refs]
    y = epi(accs, [e[...] for e in ex_refs])
    for o in out_refs:
        o[...] = y.astype(o.dtype)


def _proj(a, ws, epi, out_dtypes, *, cols=(), tiles=(), name, tm_pref=1024, tn_pref=512):
    R, K = a.shape
    N = ws[0].shape[1]
    tm = _tile(R, tm_pref)
    tn = _tile(N, tn_pref)
    in_specs = [pl.BlockSpec((tm, K), lambda i, j: (i, 0))]
    in_specs += [pl.BlockSpec((K, tn), lambda i, j: (0, j)) for _ in ws]
    in_specs += [pl.BlockSpec((1, tn), lambda i, j: (0, j)) for _ in cols]
    in_specs += [pl.BlockSpec((tm, tn), lambda i, j: (i, j)) for _ in tiles]
    vmem = 2 * tm * K * 2 + 2 * len(ws) * K * tn * 2 + (len(ws) + 2) * tm * tn * 4
    vmem += sum(2 * tm * tn * t.dtype.itemsize for t in tiles)
    vmem += sum(2 * tm * tn * jnp.dtype(d).itemsize for d in out_dtypes)
    res = pl.pallas_call(
        functools.partial(_proj_body, epi, len(ws), len(cols) + len(tiles)),
        grid=(R // tm, N // tn),
        in_specs=in_specs,
        out_specs=[pl.BlockSpec((tm, tn), lambda i, j: (i, j)) for _ in out_dtypes],
        out_shape=[jax.ShapeDtypeStruct((R, N), d) for d in out_dtypes],
        compiler_params=_cparams(2, vmem),
        name=name,
    )(a, *ws, *cols, *tiles)
    return res


def _epi_plain(accs, ex):
    return accs[0]


def _epi_group_rms(width, accs, ex):
    return _group_rms(accs[0], ex[0], width)


def _epi_glu(accs, ex):
    return accs[0] * jax.nn.sigmoid(accs[1])


def _epi_sigmoid(accs, ex):
    return jax.nn.sigmoid(accs[0])


def _epi_residual(accs, ex):
    return ex[0] + accs[0]


def _merge_body(a_ref, c_ref, m_ref, wa_ref, wc_ref, wm_ref, ga_ref, gc_ref, gm_ref, o_ref):
    ya = jnp.dot(a_ref[...], wa_ref[...], preferred_element_type=F32)
    yc = jnp.dot(c_ref[...], wc_ref[...], preferred_element_type=F32)
    ym = jnp.dot(m_ref[...], wm_ref[...], preferred_element_type=F32)
    merged = ga_ref[...].astype(F32) * ya + gc_ref[...].astype(F32) * yc + gm_ref[...].astype(F32) * ym
    o_ref[...] = merged.astype(o_ref.dtype)


def _merge(oa, oc, om, wa, wc, wm, gates):
    R = oa.shape[0]
    D = wa.shape[1]
    tm = _tile(R, 512)
    tn = _tile(D, 512)
    nt = D // tn
    row = lambda arr: pl.BlockSpec((tm, arr.shape[1]), lambda i, j: (i, 0))
    col = lambda arr: pl.BlockSpec((arr.shape[0], tn), lambda i, j: (0, j))
    gate = lambda b: pl.BlockSpec((tm, tn), lambda i, j: (i, b * nt + j))
    ksum = oa.shape[1] + oc.shape[1] + om.shape[1]
    vmem = 2 * tm * ksum * 2 + 2 * ksum * tn * 2 + 2 * 4 * tm * tn * 2 + 4 * tm * tn * 4
    return pl.pallas_call(
        _merge_body,
        grid=(R // tm, nt),
        in_specs=[row(oa), row(oc), row(om), col(wa), col(wc), col(wm), gate(0), gate(1), gate(2)],
        out_specs=pl.BlockSpec((tm, tn), lambda i, j: (i, j)),
        out_shape=jax.ShapeDtypeStruct((R, D), BF16),
        compiler_params=_cparams(2, vmem),
        name="merge",
    )(oa, oc, om, wa, wc, wm, gates, gates, gates)


def _t5_bucket(rel):
    nb = REL_BUCKETS // 2
    max_exact = nb // 2
    n = jnp.abs(rel)
    nf = jnp.maximum(n, max_exact).astype(F32)
    large = max_exact + (jnp.log(nf / max_exact) / math.log(REL_MAX_DIST / max_exact)
                         * (nb - max_exact)).astype(jnp.int32)
    large = jnp.minimum(large, nb - 1)
    return jnp.where(rel > 0, nb, 0) + jnp.where(n < max_exact, n, large)


def _bias_window(table, q0, nq, k0, nk):
    span = nq + nk - 1
    rel = (k0 - q0) - (nq - 1) + np.arange(span, dtype=np.int32)
    far = table[REL_BUCKETS // 2 - 1].astype(F32)
    v = ((table[_t5_bucket(jnp.asarray(rel))].astype(F32) - far) * LOG2E).T
    h = v.shape[0]
    rows = jnp.tile(jnp.pad(v, ((0, 0), (0, 1))), (1, nq))[:, :nq * span].reshape(h, nq, span)
    b = rows[:, :, nq - 1:nq - 1 + nk]
    qpos = q0 + np.arange(nq)
    kpos = k0 + np.arange(nk)
    visible = (kpos[None, :] // CHUNK) <= (qpos[:, None] // CHUNK)
    return jnp.where(jnp.asarray(visible)[None], b, NEG_INF)


def _lambda(lam_ref, lam_init):
    lv = lam_ref[...]
    s1 = jnp.sum(lv[0:1] * lv[1:2], axis=-1, keepdims=True)
    s2 = jnp.sum(lv[2:3] * lv[3:4], axis=-1, keepdims=True)
    return jnp.exp(s1) - jnp.exp(s2) + lam_init


def _attn_prompt_body(tq, tk, d, lam_init, qt_ref, k_ref, vt_ref, bias_ref, lam_ref, g_ref, o_ref,
                      s_scr, m_scr, l_scr, acc_scr):
    i = pl.program_id(1)
    r = tq // tk
    first_near = r * i - 1
    m_scr[...] = jnp.full(m_scr.shape, NEG_INF, F32)
    l_scr[...] = jnp.zeros(l_scr.shape, F32)
    acc_scr[...] = jnp.zeros(acc_scr.shape, F32)

    def scores(j, slot=None):
        row0 = pl.multiple_of(j * tk, tk)
        out = [jnp.dot(k_ref[pl.ds(row0, tk), s * d:(s + 1) * d], qt_ref[0, 0, s * d:(s + 1) * d, :],
                       preferred_element_type=F32) for s in range(2)]
        if slot is None:
            return out
        bias = bias_ref[0, slot]
        return [o + bias for o in out]

    def softmax_pv(j, sc):
        vt = vt_ref[0, j]
        for s in range(2):
            m_old = m_scr[s]
            m_new = jnp.maximum(m_old, jnp.max(sc[s], axis=0, keepdims=True))
            alpha = jnp.exp2(m_old - m_new)
            p = jnp.exp2(sc[s] - m_new)
            l_scr[s] = alpha * l_scr[s] + jnp.sum(p, axis=0, keepdims=True)
            acc_scr[s] = alpha * acc_scr[s] + jnp.dot(vt, p.astype(BF16), preferred_element_type=F32)
            m_scr[s] = m_new

    def step(j, slot=None):
        nxt = scores(j + 1, slot)
        softmax_pv(j, [s_scr[0], s_scr[1]])
        for s in range(2):
            s_scr[s] = nxt[s]

    first = scores(0, jnp.clip(1 - first_near, 0, r + 1))
    for s in range(2):
        s_scr[s] = first[s]

    def far(j, carry):
        step(j)
        return carry

    lax.fori_loop(0, jnp.maximum(first_near - 1, 0), far, 0)

    for n in range(r + 1):
        @pl.when(first_near + n >= 1)
        def _(n=n):
            step(first_near + n - 1, 1 + n)

    softmax_pv(first_near + r, [s_scr[0], s_scr[1]])

    lam = _lambda(lam_ref, lam_init)
    o = acc_scr[0] / l_scr[0] - lam * (acc_scr[1] / l_scr[1])
    o = o * lax.rsqrt(jnp.mean(o * o, axis=0, keepdims=True) + EPS) * g_ref[...] * (1.0 - lam_init)
    o_ref[...] = o.T.astype(o_ref.dtype)


def _attn_prompt(q, k, v, table, lam_vecs, subln, lam_init, n_heads, tq_pref=1024, tk_pref=512):
    T = q.shape[0]
    w = q.shape[1] // n_heads
    d = w // 2
    tq = _tile(T, tq_pref)
    tk = _tile(tq, tk_pref)
    r = tq // tk
    assert tk % CHUNK == 0 and tk >= REL_MAX_DIST, (tk, "key tiles two back must be at constant bias")
    qt = q.reshape(T // tq, tq, n_heads, w).transpose(2, 0, 3, 1)
    vt = v.reshape(T // tk, tk, n_heads, w).transpose(2, 0, 3, 1)
    near = [_bias_window(table, tk, tq, n * tk, tk) for n in range(r + 1)]
    bias = jnp.stack([jnp.zeros_like(near[0])] + near, axis=1).transpose(0, 1, 3, 2)
    vmem = 2 * T * w * 2 + (r + 2) * tk * tq * 4 + 4 * tq * w * 2 + 2 * tq * w * 4 + 10 * tk * tq * 4
    return pl.pallas_call(
        functools.partial(_attn_prompt_body, tq, tk, d, lam_init),
        grid=(n_heads, T // tq),
        in_specs=[pl.BlockSpec((1, 1, w, tq), lambda h, i: (h, i, 0, 0)),
                  pl.BlockSpec((T, w), lambda h, i: (0, h), pipeline_mode=pl.Buffered(1)),
                  pl.BlockSpec((1, T // tk, w, tk), lambda h, i: (h, 0, 0, 0), pipeline_mode=pl.Buffered(1)),
                  pl.BlockSpec((1, r + 2, tk, tq), lambda h, i: (h, 0, 0, 0), pipeline_mode=pl.Buffered(1)),
                  pl.BlockSpec((4, d), lambda h, i: (0, 0)),
                  pl.BlockSpec((w, 1), lambda h, i: (0, 0))],
        out_specs=pl.BlockSpec((tq, w), lambda h, i: (i, h)),
        out_shape=jax.ShapeDtypeStruct((T, n_heads * w), BF16),
        scratch_shapes=[pltpu.VMEM((2, tk, tq), F32), pltpu.VMEM((2, 1, tq), F32), pltpu.VMEM((2, 1, tq), F32),
                        pltpu.VMEM((2, w, tq), F32)],
        compiler_params=_cparams(2, vmem),
        name="attn_prompt",
    )(qt, k, vt, bias, lam_vecs, subln.reshape(w, 1))


def _attn_sample_body(d, lam_init, q_ref, ck_ref, cv_ref, kn_ref, vn_ref, mb_ref, mbn_ref,
                      lam_ref, g_ref, o_ref, m_scr, l_scr, acc_scr):
    pj = pl.program_id(1)
    last = pl.num_programs(1) - 1
    nt = (((1,), (1,)), ((), ()))

    @pl.when(pj == 0)
    def _():
        m_scr[...] = jnp.full(m_scr.shape, NEG_INF, F32)
        l_scr[...] = jnp.zeros(l_scr.shape, F32)
        acc_scr[...] = jnp.zeros(acc_scr.shape, F32)

    def update(kb, vb, mb):
        for s in range(2):
            g = lax.dot_general(q_ref[0, s], kb[:, s * d:(s + 1) * d], nt, preferred_element_type=F32) + mb
            m_old = m_scr[s]
            m_new = jnp.maximum(m_old, jnp.max(g, axis=-1, keepdims=True))
            alpha = jnp.exp2(m_old - m_new)
            p = jnp.exp2(g - m_new)
            l_scr[s] = alpha * l_scr[s] + jnp.sum(p, axis=-1, keepdims=True)
            acc_scr[s] = alpha * acc_scr[s] + jnp.dot(p.astype(BF16), vb, preferred_element_type=F32)
            m_scr[s] = m_new

    slot = (pj == last).astype(jnp.int32)
    rows = ck_ref.shape[1]
    ch = min(rows, 1024)
    for r0 in range(0, rows, ch):
        update(ck_ref[0, r0:r0 + ch, :].astype(BF16), cv_ref[0, r0:r0 + ch, :].astype(BF16),
               mb_ref[slot, :, r0:r0 + ch])

    @pl.when(pj == last)
    def _():
        update(kn_ref[0], vn_ref[0], mbn_ref[...])
        lam = _lambda(lam_ref, lam_init)
        o = acc_scr[0] / l_scr[0] - lam * (acc_scr[1] / l_scr[1])
        o_ref[0] = (_rms(o, g_ref[...]) * (1.0 - lam_init)).astype(o_ref.dtype)


def _head_mask_bias(bias, n_heads):
    same = jnp.eye(n_heads, dtype=bool)[:, None, None, :]
    full = jnp.where(same, bias[..., None], NEG_INF)
    return full.reshape(bias.shape[0] * bias.shape[1], bias.shape[2] * n_heads)


def _attn_sample(q, kn, vn, cache_k, cache_v, table, lam_vecs, subln, lam_init, n_heads):
    B, S, HW = q.shape
    P = cache_k.shape[1]
    w = HW // n_heads
    d = w // 2
    tp = _tile(P, 256)
    hs = n_heads * S
    assert tp >= REL_MAX_DIST or tp == P, "only the newest cache tile may carry a non-constant bias"
    near = _bias_window(table, P, S, P - tp, tp)
    mb = jnp.stack([_head_mask_bias(jnp.zeros_like(near), n_heads), _head_mask_bias(near, n_heads)])
    mbn = _head_mask_bias(_bias_window(table, P, S, P, S), n_heads)
    qs = q.reshape(B, S, n_heads, 2, d).transpose(0, 3, 2, 1, 4).reshape(B, 2, hs, d)
    new = pl.BlockSpec((1, S * n_heads, w), lambda b, j: (b, 0, 0))
    old = pl.BlockSpec((1, tp * n_heads, w), lambda b, j: (b, j, 0))
    vmem = (2 * 2 * tp * HW * 4 + 2 * tp * HW * 2 + 2 * hs * tp * n_heads * 4 + 4 * hs * tp * n_heads * 4
            + 4 * hs * w * 4)
    o = pl.pallas_call(
        functools.partial(_attn_sample_body, d, lam_init),
        grid=(B, P // tp),
        in_specs=[pl.BlockSpec((1, 2, hs, d), lambda b, j: (b, 0, 0, 0)),
                  old, old, new, new,
                  pl.BlockSpec((2, hs, tp * n_heads), lambda b, j: (0, 0, 0), pipeline_mode=pl.Buffered(1)),
                  pl.BlockSpec((hs, S * n_heads), lambda b, j: (0, 0)),
                  pl.BlockSpec((4, d), lambda b, j: (0, 0)),
                  pl.BlockSpec((1, w), lambda b, j: (0, 0))],
        out_specs=pl.BlockSpec((1, hs, w), lambda b, j: (b, 0, 0)),
        out_shape=jax.ShapeDtypeStruct((B, hs, w), BF16),
        scratch_shapes=[pltpu.VMEM((2, hs, 1), F32), pltpu.VMEM((2, hs, 1), F32), pltpu.VMEM((2, hs, w), F32)],
        compiler_params=_cparams(2, vmem),
        name="attn_sample",
    )(qs, cache_k.reshape(B, P * n_heads, w), cache_v.reshape(B, P * n_heads, w),
      kn.reshape(B, S * n_heads, w), vn.reshape(B, S * n_heads, w), mb, mbn, lam_vecs, subln.reshape(1, w))
    return o.reshape(B, n_heads, S, w).transpose(0, 2, 1, 3).reshape(B, S, HW)


def _conv_body(width, hist_ref, halo_ref, main_ref, w_ref, b_ref, g_ref, beta_ref, o_ref,
               win_scr, sh_scr, conv_scr):
    i = pl.program_id(1)
    tm = main_ref.shape[1]
    C = main_ref.shape[2]

    @pl.when(i == 0)
    def _():
        win_scr[0:CONV_HALO, :] = hist_ref[0]

    @pl.when(i > 0)
    def _():
        win_scr[0:CONV_HALO, :] = halo_ref[0]

    win_scr[CONV_HALO:CONV_HALO + tm, :] = main_ref[0]
    first = CONV_HALO - (width - 1)
    for ph in range(min(8, width)):
        rows = tm + 8 * ((width - 1 - ph) // 8)
        sh_scr[ph, 0:rows, :] = win_scr[first + ph:first + ph + rows, :]
    rc = min(tm, 32)
    cc = min(C, 512)
    for r0 in range(0, tm, rc):
        for c0 in range(0, C, cc):
            acc = jnp.broadcast_to(b_ref[:, c0:c0 + cc], (rc, cc))
            for tap in range(width):
                ph, off = tap % 8, r0 + 8 * (tap // 8)
                acc = acc + sh_scr[ph, off:off + rc, c0:c0 + cc] * w_ref[tap:tap + 1, c0:c0 + cc]
            conv_scr[r0:r0 + rc, c0:c0 + cc] = acc
    c = conv_scr[...]
    xc = c - jnp.mean(c, axis=-1, keepdims=True)
    y = xc * lax.rsqrt(jnp.mean(xc * xc, axis=-1, keepdims=True) + EPS) * g_ref[...] + beta_ref[...]
    o_ref[0] = (y * jax.nn.sigmoid(y)).astype(o_ref.dtype)


def _conv_module(u, hist, w, b, g, beta):
    B, T, C = u.shape
    width = w.shape[0]
    tm = _tile(T, 64)
    nt = T // tm
    assert width - 1 <= CONV_HALO and (tm % CONV_HALO == 0 or nt == 1)
    hist_pad = jnp.concatenate([jnp.zeros((B, CONV_HALO - (width - 1), C), F32), hist], axis=1)
    halo_src = u if nt > 1 else hist_pad
    halo_per_tile = tm // CONV_HALO if nt > 1 else 0
    vec = lambda: pl.BlockSpec((1, C), lambda bb, i: (0, 0))
    sh_rows = tm + 8 * ((width - 1) // 8)
    return pl.pallas_call(
        functools.partial(_conv_body, width),
        grid=(B, nt),
        in_specs=[pl.BlockSpec((1, CONV_HALO, C), lambda bb, i: (bb, 0, 0)),
                  pl.BlockSpec((1, CONV_HALO, C), lambda bb, i: (bb, jnp.maximum(i * halo_per_tile - 1, 0), 0)),
                  pl.BlockSpec((1, tm, C), lambda bb, i: (bb, i, 0)),
                  pl.BlockSpec((width, C), lambda bb, i: (0, 0)),
                  vec(), vec(), vec()],
        out_specs=pl.BlockSpec((1, tm, C), lambda bb, i: (bb, i, 0)),
        out_shape=jax.ShapeDtypeStruct((B, T, C), BF16),
        scratch_shapes=[pltpu.VMEM((CONV_HALO + tm, C), F32), pltpu.VMEM((min(8, width), sh_rows, C), F32),
                        pltpu.VMEM((tm, C), F32)],
        compiler_params=_cparams(2, (16 * (CONV_HALO + tm) + 4 * 8 * sh_rows) * C),
        name="conv_module",
    )(hist_pad, halo_src, u, w, b.reshape(1, C), g.reshape(1, C), beta.reshape(1, C))


def _mem_attn_body(n_heads, dq, dv, q_ref, k_ref, v_ref, o_ref):
    nt = (((1,), (1,)), ((), ()))
    outs = []
    for h in range(n_heads):
        qh = q_ref[0, :, h * dq:(h + 1) * dq]
        kh = k_ref[0, :, h * dq:(h + 1) * dq].astype(BF16)
        vh = v_ref[0, :, h * dv:(h + 1) * dv].astype(BF16)
        s = lax.dot_general(qh, kh, nt, preferred_element_type=F32)
        p = jnp.exp2(s - jnp.max(s, axis=-1, keepdims=True))
        l = jnp.sum(p, axis=-1, keepdims=True)
        outs.append(jnp.dot(p.astype(BF16), vh, preferred_element_type=F32) / l)
    o_ref[0] = jnp.concatenate(outs, axis=-1).astype(o_ref.dtype)


def _mem_attn(mq, mem_k, mem_v, n_heads):
    B, T, QW = mq.shape
    M = mem_k.shape[1]
    VW = mem_v.shape[2]
    tm = _tile(T, 512)
    return pl.pallas_call(
        functools.partial(_mem_attn_body, n_heads, QW // n_heads, VW // n_heads),
        grid=(B, T // tm),
        in_specs=[pl.BlockSpec((1, tm, QW), lambda b, i: (b, i, 0)),
                  pl.BlockSpec((1, M, QW), lambda b, i: (b, 0, 0)),
                  pl.BlockSpec((1, M, VW), lambda b, i: (b, 0, 0))],
        out_specs=pl.BlockSpec((1, tm, VW), lambda b, i: (b, i, 0)),
        out_shape=jax.ShapeDtypeStruct((B, T, VW), BF16),
        compiler_params=_cparams(2, 2 * (tm * QW * 2 + M * (QW + VW) * 4 + tm * VW * 2) + 3 * tm * VW * 4),
        name="mem_attn",
    )(mq, mem_k, mem_v)


def _layer_group(x, past, mem_k, mem_v, table, lam_init, p, dims):
    B, T, D = x.shape
    H, d, MH = dims["H"], dims["d"], dims["MH"]
    R = B * T
    xf = x.reshape(R, D)

    x1, h = _ffn(xf, p["ffn1_norm"], p["ffn1_w_gate"], p["ffn1_w_up"], p["ffn1_w_down"], p["mix_norm"])

    (q,) = _proj(h, [p["w_q"]], functools.partial(_epi_group_rms, d), [BF16], cols=[p["q_gain"]], name="proj_q")
    k32, k16 = _proj(h, [p["w_k"]], functools.partial(_epi_group_rms, d), [F32, BF16], cols=[p["k_gain"]],
                     name="proj_k")
    v32, v16 = _proj(h, [p["w_v"]], _epi_plain, [F32, BF16], name="proj_v")
    (u,) = _proj(h, [p["w_glu_a"], p["w_glu_b"]], _epi_glu, [F32], name="proj_glu")
    (mq,) = _proj(h, [p["w_mq"]], functools.partial(_epi_group_rms, dims["dmq"]), [BF16], cols=[p["mq_gain"]],
                  name="proj_mq")
    (gates,) = _proj(h, [p["w_gates"]], _epi_sigmoid, [BF16], name="proj_gates")

    HW = H * 2 * d
    C = u.shape[1]
    width = p["conv_dw"].shape[0]
    if past is None:
        assert B == 1
        oa = _attn_prompt(q, k16, v16, table, p["lam_vecs"], p["attn_subln"], lam_init, H)
        hist = jnp.zeros((B, width - 1, C), F32)
    else:
        cache_k, cache_v, hist = past
        oa = _attn_sample(q.reshape(B, T, HW), k16.reshape(B, T, HW), v16.reshape(B, T, HW), cache_k, cache_v,
                          table, p["lam_vecs"], p["attn_subln"], lam_init, H).reshape(R, HW)

    u3 = u.reshape(B, T, C)
    conv_state = (u3[:, T - (width - 1):] if T >= width - 1
                  else jnp.concatenate([hist[:, T:], u3], axis=1))
    oc = _conv_module(u3, hist, p["conv_dw"], p["conv_dw_bias"], p["conv_ln_g"], p["conv_ln_b"]).reshape(R, C)

    M = mem_k.shape[1]
    om = _mem_attn(mq.reshape(B, T, -1), mem_k.reshape(B, M, -1), mem_v.reshape(B, M, -1), MH)
    om = om.reshape(R, -1)

    merged = _merge(oa, oc, om, p["w_attn_out"], p["w_conv_out"], p["w_mem_out"], gates)
    (x2,) = _proj(merged, [p["w_out"]], _epi_residual, [F32], tiles=[x1], name="proj_out")
    y = _ffn(x2, p["ffn2_norm"], p["ffn2_w_gate"], p["ffn2_w_up"], p["ffn2_w_down"])
    return (y.reshape(B, T, D), k32.reshape(B, T, H, 2 * d), v32.reshape(B, T, H, 2 * d), conv_state)


def _memory_kv(mem, mem_norm, w_k, w_v, k_gain, dims):
    B, M, D = mem.shape
    hm = _rmsnorm(mem.reshape(B * M, D), mem_norm)
    (k,) = _proj(hm, [w_k], functools.partial(_epi_group_rms, dims["dmq"]), [F32], cols=[k_gain], name="mem_k")
    (v,) = _proj(hm, [w_v], _epi_plain, [F32], name="mem_v")
    MH = dims["MH"]
    return k.reshape(B, M, MH, -1), v.reshape(B, M, MH, -1)


def kernel(x_prompt, x_sample, cache_attn_k, cache_attn_v, cache_conv, cache_mem_k, cache_mem_v, mem_prompt, rel_bias_table, ffn1_norm, ffn1_w_gate, ffn1_w_up, ffn1_w_down, mix_norm, w_in, attn_q_norm, attn_k_norm, lambda_q1, lambda_k1, lambda_q2, lambda_k2, attn_subln, w_attn_out, conv_dw, conv_dw_bias, conv_ln_g, conv_ln_b, w_conv_out, mem_norm, w_mem_kv, mem_q_norm, mem_k_norm, w_mem_out, w_out, ffn2_norm, ffn2_w_gate, ffn2_w_up, ffn2_w_down):
    depth = w_in.shape[0]
    D = x_prompt.shape[-1]
    H = cache_attn_k.shape[3]
    d = attn_q_norm.shape[-1]
    C = conv_dw.shape[-1]
    MH = cache_mem_k.shape[3]
    dmq = cache_mem_k.shape[4]
    dims = {"H": H, "d": d, "MH": MH, "dmq": dmq}
    c_q = H * 2 * d
    c_k, c_v = 2 * c_q, 3 * c_q
    c_conv = c_v + 2 * C
    c_memq = c_conv + MH * dmq
    bf = lambda a: a.astype(BF16)

    yp, ys = x_prompt, x_sample
    outs = [[] for _ in range(8)]
    for l in range(depth):
        lam_init = 0.8 - 0.6 * math.exp(-0.3 * l)
        wi = w_in[l]
        p = {
            "ffn1_norm": ffn1_norm[l], "ffn1_w_gate": bf(ffn1_w_gate[l]), "ffn1_w_up": bf(ffn1_w_up[l]),
            "ffn1_w_down": bf(ffn1_w_down[l]), "mix_norm": mix_norm[l],
            "w_q": bf(wi[:, :c_q]), "w_k": bf(wi[:, c_q:c_k]), "w_v": bf(wi[:, c_k:c_v]),
            "w_glu_a": bf(wi[:, c_v:c_v + C]), "w_glu_b": bf(wi[:, c_v + C:c_conv]),
            "w_mq": bf(wi[:, c_conv:c_memq]), "w_gates": bf(wi[:, c_memq:]),
            "q_gain": jnp.tile(attn_q_norm[l] * (d ** -0.5 * LOG2E), 2 * H).reshape(1, c_q),
            "k_gain": jnp.tile(attn_k_norm[l], 2 * H).reshape(1, c_q),
            "mq_gain": jnp.tile(mem_q_norm[l] * (dmq ** -0.5 * LOG2E), MH).reshape(1, MH * dmq),
            "lam_vecs": jnp.stack([lambda_q1[l], lambda_k1[l], lambda_q2[l], lambda_k2[l]]).astype(F32),
            "attn_subln": attn_subln[l], "w_attn_out": bf(w_attn_out[l]),
            "conv_dw": conv_dw[l], "conv_dw_bias": conv_dw_bias[l],
            "conv_ln_g": conv_ln_g[l], "conv_ln_b": conv_ln_b[l], "w_conv_out": bf(w_conv_out[l]),
            "w_mem_out": bf(w_mem_out[l]), "w_out": bf(w_out[l]),
            "ffn2_norm": ffn2_norm[l], "ffn2_w_gate": bf(ffn2_w_gate[l]), "ffn2_w_up": bf(ffn2_w_up[l]),
            "ffn2_w_down": bf(ffn2_w_down[l]),
        }
        wkv = w_mem_kv[l]
        mk, mv = _memory_kv(mem_prompt, mem_norm[l], bf(wkv[:, :MH * dmq]), bf(wkv[:, MH * dmq:]),
                            jnp.tile(mem_k_norm[l], MH).reshape(1, MH * dmq), dims)
        yp, kp, vp, cp = _layer_group(yp, None, mk, mv, rel_bias_table, lam_init, p, dims)
        ys, ks, vs, cs = _layer_group(ys, (cache_attn_k[l], cache_attn_v[l], cache_conv[l]),
                                      cache_mem_k[l], cache_mem_v[l], rel_bias_table, lam_init, p, dims)
        for lst, val in zip(outs, (kp, vp, cp, mk, mv, ks, vs, cs)):
            lst.append(val)
    return (yp, ys) + tuple(jnp.stack(o) for o in outs)
```

```python
import functools
import math

import numpy as np
import jax
import jax.numpy as jnp
from jax import lax
from jax.experimental import pallas as pl
from jax.experimental.pallas import tpu as pltpu

F32 = jnp.float32
BF16 = jnp.bfloat16

EPS = 1e-6
NEG_INF = -1e30
CHUNK = 64
REL_BUCKETS = 32
REL_MAX_DIST = 128
LOG2E = math.log2(math.e)

V7X_VMEM_LIMIT_MAX = 56 * 1024 * 1024
V7X_LANES = 128
V7X_MXU_DIM = 256
CONV_HALO = 32


def _cparams(n_axes, vmem_bytes):
    limit = min(max(int(vmem_bytes * 1.2) + (6 << 20), 32 << 20), V7X_VMEM_LIMIT_MAX)
    return pltpu.CompilerParams(dimension_semantics=("arbitrary",) * n_axes,
                                vmem_limit_bytes=limit)


def _tile(n, pref):
    if n <= pref:
        return n
    t = pref
    while n % t:
        t //= 2
    return t


def _rms(x, g):
    return x * lax.rsqrt(jnp.mean(x * x, axis=-1, keepdims=True) + EPS) * g


def _group_rms(x, g, width):
    parts = []
    for c in range(0, x.shape[-1], width):
        parts.append(_rms(x[:, c:c + width], g[:, c:c + width]))
    return parts[0] if len(parts) == 1 else jnp.concatenate(parts, axis=-1)


def _rmsnorm_body(x_ref, g_ref, o_ref):
    o_ref[...] = _rms(x_ref[...], g_ref[...]).astype(o_ref.dtype)


def _rmsnorm(x, g):
    R, D = x.shape
    tm = _tile(R, 256)
    return pl.pallas_call(
        _rmsnorm_body,
        grid=(R // tm,),
        in_specs=[pl.BlockSpec((tm, D), lambda i: (i, 0)),
                  pl.BlockSpec((1, D), lambda i: (0, 0))],
        out_specs=pl.BlockSpec((tm, D), lambda i: (i, 0)),
        out_shape=jax.ShapeDtypeStruct((R, D), BF16),
        compiler_params=_cparams(1, 2 * tm * D * 6),
        name="rmsnorm",
    )(x, g.reshape(1, D))


def _ffn_body(emit_norm, nj, x_ref, g_ref, wg_ref, wu_ref, wd_ref, *rest):
    if emit_norm:
        g2_ref, out_ref, hn_ref, xn_scr = rest
    else:
        out_ref, xn_scr = rest
    j = pl.program_id(1)

    tm = out_ref.shape[0]
    rc = min(tm, 64)

    @pl.when(j == 0)
    def _():
        for r0 in range(0, tm, rc):
            x = x_ref[r0:r0 + rc, :]
            xn_scr[r0:r0 + rc, :] = _rms(x, g_ref[...]).astype(BF16)
            out_ref[r0:r0 + rc, :] = x

    xn = xn_scr[...]
    a = jnp.dot(xn, wg_ref[...], preferred_element_type=F32)
    b = jnp.dot(xn, wu_ref[...], preferred_element_type=F32)
    h = (a * jax.nn.sigmoid(a) * b * 0.5).astype(BF16)
    D = out_ref.shape[1]
    cw = min(D, 1024)
    for c0 in range(0, D, cw):
        out_ref[:, c0:c0 + cw] += jnp.dot(h, wd_ref[:, c0:c0 + cw], preferred_element_type=F32)

    if emit_norm:
        @pl.when(j == nj - 1)
        def _():
            for r0 in range(0, tm, rc):
                hn_ref[r0:r0 + rc, :] = _rms(out_ref[r0:r0 + rc, :], g2_ref[...]).astype(BF16)


def _ffn(x, g, wg, wu, wd, g2=None):
    R, D = x.shape
    Fdim = wg.shape[1]
    tm = _tile(R, 512)
    tf = _tile(Fdim, V7X_MXU_DIM)
    nj = Fdim // tf
    emit_norm = g2 is not None
    in_specs = [pl.BlockSpec((tm, D), lambda i, j: (i, 0), pipeline_mode=pl.Buffered(1)),
                pl.BlockSpec((1, D), lambda i, j: (0, 0)),
                pl.BlockSpec((D, tf), lambda i, j: (0, j)),
                pl.BlockSpec((D, tf), lambda i, j: (0, j)),
                pl.BlockSpec((tf, D), lambda i, j: (j, 0))]
    args = [x, g.reshape(1, D), wg, wu, wd]
    out_specs = [pl.BlockSpec((tm, D), lambda i, j: (i, 0))]
    out_shape = [jax.ShapeDtypeStruct((R, D), F32)]
    vmem = tm * D * 4 + 2 * tm * D * 4 + tm * D * 2 + 2 * 3 * D * tf * 2 + 4 * tm * tf * 4
    if emit_norm:
        in_specs.append(pl.BlockSpec((1, D), lambda i, j: (0, 0)))
        args.append(g2.reshape(1, D))
        out_specs.append(pl.BlockSpec((tm, D), lambda i, j: (i, 0)))
        out_shape.append(jax.ShapeDtypeStruct((R, D), BF16))
        vmem += 2 * tm * D * 2
    res = pl.pallas_call(
        functools.partial(_ffn_body, emit_norm, nj),
        grid=(R // tm, nj),
        in_specs=in_specs,
        out_specs=out_specs,
        out_shape=out_shape,
        scratch_shapes=[pltpu.VMEM((tm, D), BF16)],
        compiler_params=_cparams(2, vmem),
        name="ffn_norm" if emit_norm else "ffn",
    )(*args)
    return res if emit_norm else res[0]


def _proj_body(epi, n_w, n_ex, a_ref, *refs):
    w_refs = refs[:n_w]
    ex_refs = refs[n_w:n_w + n_ex]
    out_refs = refs[n_w + n_ex:]
    a = a_ref[...]
    accs = [jnp.dot(a, w[...], preferred_element_type=F32) for w in w_refs]
    y = epi(accs, [e[...] for e in ex_refs])
    for o in out_refs:
        o[...] = y.astype(o.dtype)


def _proj(a, ws, epi, out_dtypes, *, cols=(), tiles=(), name, tm_pref=1024, tn_pref=512):
    R, K = a.shape
    ws = [w if isinstance(w, tuple) else (w, 0, w.shape[1]) for w in ws]
    N = ws[0][2]
    tm = _tile(R, tm_pref)
    tn = _tile(N, tn_pref)
    ws = [(w, c0 // tn) if c0 % tn == 0 else (w[:, c0:c0 + N], 0) for w, c0, _ in ws]
    in_specs = [pl.BlockSpec((tm, K), lambda i, j: (i, 0))]
    in_specs += [pl.BlockSpec((K, tn), lambda i, j, j0=j0: (0, j0 + j)) for _, j0 in ws]
    in_specs += [pl.BlockSpec((1, tn), lambda i, j: (0, j)) for _ in cols]
    in_specs += [pl.BlockSpec((tm, tn), lambda i, j: (i, j)) for _ in tiles]
    vmem = 2 * tm * K * 2 + 2 * len(ws) * K * tn * 2 + (len(ws) + 2) * tm * tn * 4
    vmem += sum(2 * tm * tn * t.dtype.itemsize for t in tiles)
    vmem += sum(2 * tm * tn * jnp.dtype(d).itemsize for d in out_dtypes)
    res = pl.pallas_call(
        functools.partial(_proj_body, epi, len(ws), len(cols) + len(tiles)),
        grid=(R // tm, N // tn),
        in_specs=in_specs,
        out_specs=[pl.BlockSpec((tm, tn), lambda i, j: (i, j)) for _ in out_dtypes],
        out_shape=[jax.ShapeDtypeStruct((R, N), d) for d in out_dtypes],
        compiler_params=_cparams(2, vmem),
        name=name,
    )(a, *[w for w, _ in ws], *cols, *tiles)
    return res


def _epi_plain(accs, ex):
    return accs[0]


def _epi_group_rms(width, accs, ex):
    return _group_rms(accs[0], ex[0], width)


def _epi_glu(accs, ex):
    return accs[0] * jax.nn.sigmoid(accs[1])


def _epi_sigmoid(accs, ex):
    return jax.nn.sigmoid(accs[0])


def _epi_residual(accs, ex):
    return ex[0] + accs[0]


def _merge_body(a_ref, c_ref, m_ref, wa_ref, wc_ref, wm_ref, ga_ref, gc_ref, gm_ref, o_ref):
    ya = jnp.dot(a_ref[...], wa_ref[...], preferred_element_type=F32)
    yc = jnp.dot(c_ref[...], wc_ref[...], preferred_element_type=F32)
    ym = jnp.dot(m_ref[...], wm_ref[...], preferred_element_type=F32)
    merged = ga_ref[...].astype(F32) * ya + gc_ref[...].astype(F32) * yc + gm_ref[...].astype(F32) * ym
    o_ref[...] = merged.astype(o_ref.dtype)


def _merge(oa, oc, om, wa, wc, wm, gates):
    R = oa.shape[0]
    D = wa.shape[1]
    tm = _tile(R, 512)
    tn = _tile(D, 512)
    nt = D // tn
    row = lambda arr: pl.BlockSpec((tm, arr.shape[1]), lambda i, j: (i, 0))
    col = lambda arr: pl.BlockSpec((arr.shape[0], tn), lambda i, j: (0, j))
    gate = lambda b: pl.BlockSpec((tm, tn), lambda i, j: (i, b * nt + j))
    ksum = oa.shape[1] + oc.shape[1] + om.shape[1]
    vmem = 2 * tm * ksum * 2 + 2 * ksum * tn * 2 + 2 * 4 * tm * tn * 2 + 4 * tm * tn * 4
    return pl.pallas_call(
        _merge_body,
        grid=(R // tm, nt),
        in_specs=[row(oa), row(oc), row(om), col(wa), col(wc), col(wm), gate(0), gate(1), gate(2)],
        out_specs=pl.BlockSpec((tm, tn), lambda i, j: (i, j)),
        out_shape=jax.ShapeDtypeStruct((R, D), BF16),
        compiler_params=_cparams(2, vmem),
        name="merge",
    )(oa, oc, om, wa, wc, wm, gates, gates, gates)


def _t5_bucket(rel):
    nb = REL_BUCKETS // 2
    max_exact = nb // 2
    n = jnp.abs(rel)
    nf = jnp.maximum(n, max_exact).astype(F32)
    large = max_exact + (jnp.log(nf / max_exact) / math.log(REL_MAX_DIST / max_exact)
                         * (nb - max_exact)).astype(jnp.int32)
    large = jnp.minimum(large, nb - 1)
    return jnp.where(rel > 0, nb, 0) + jnp.where(n < max_exact, n, large)


def _bias_window(table, q0, nq, k0, nk):
    far = table[REL_BUCKETS // 2 - 1].astype(F32)

    def toeplitz(off, n, m):
        span = n + m - 1
        rel = off - (n - 1) + np.arange(span, dtype=np.int32)
        v = ((table[_t5_bucket(jnp.asarray(rel))].astype(F32) - far) * LOG2E).T
        rows = jnp.tile(jnp.pad(v, ((0, 0), (0, 1))), (1, n))[:, :n * span].reshape(v.shape[0], n, span)
        return rows[:, :, n - 1:n - 1 + m]

    g = REL_MAX_DIST
    if nq % g or nk % g or (k0 - q0) % g:
        b = toeplitz(k0 - q0, nq, nk)
    else:
        a_idx = np.arange(nq // g)[:, None]
        b_idx = np.arange(nk // g)[None, :]
        diag = (k0 - q0) // g + b_idx - a_idx
        b = 0.0
        for c in (-1, 0, 1):
            place = jnp.asarray((diag == c).astype(np.float32))
            b = b + place[None, :, None, :, None] * toeplitz(c * g, g, g)[:, None, :, None, :]
        b = b.reshape(b.shape[0], nq, nk)
    qpos = q0 + np.arange(nq)
    kpos = k0 + np.arange(nk)
    visible = (kpos[None, :] // CHUNK) <= (qpos[:, None] // CHUNK)
    return jnp.where(jnp.asarray(visible)[None], b, NEG_INF)


def _lambda(lam_ref, lam_init):
    lv = lam_ref[...]
    s1 = jnp.sum(lv[0:1] * lv[1:2], axis=-1, keepdims=True)
    s2 = jnp.sum(lv[2:3] * lv[3:4], axis=-1, keepdims=True)
    return jnp.exp(s1) - jnp.exp(s2) + lam_init


def _attn_prompt_body(tq, tk, d, lam_init, qt_ref, k_ref, vt_ref, bias_ref, lam_ref, g_ref, o_ref,
                      s_scr, m_scr, l_scr, acc_scr):
    i = pl.program_id(1)
    r = tq // tk
    first_near = r * i - 1
    m_scr[...] = jnp.full(m_scr.shape, NEG_INF, F32)
    l_scr[...] = jnp.zeros(l_scr.shape, F32)
    acc_scr[...] = jnp.zeros(acc_scr.shape, F32)

    def scores(j, slot=None):
        row0 = pl.multiple_of(j * tk, tk)
        out = [jnp.dot(k_ref[pl.ds(row0, tk), s * d:(s + 1) * d], qt_ref[0, 0, s * d:(s + 1) * d, :],
                       preferred_element_type=F32) for s in range(2)]
        if slot is None:
            return out
        bias = bias_ref[0, slot]
        return [o + bias for o in out]

    def softmax_pv(j, sc):
        vt = vt_ref[0, j]
        for s in range(2):
            m_old = m_scr[s]
            m_new = jnp.maximum(m_old, jnp.max(sc[s], axis=0, keepdims=True))
            alpha = jnp.exp2(m_old - m_new)
            p = jnp.exp2(sc[s] - m_new)
            l_scr[s] = alpha * l_scr[s] + jnp.sum(p, axis=0, keepdims=True)
            acc_scr[s] = alpha * acc_scr[s] + jnp.dot(vt, p.astype(BF16), preferred_element_type=F32)
            m_scr[s] = m_new

    def step(j, slot=None):
        nxt = scores(j + 1, slot)
        softmax_pv(j, [s_scr[0], s_scr[1]])
        for s in range(2):
            s_scr[s] = nxt[s]

    first = scores(0, jnp.clip(1 - first_near, 0, r + 1))
    for s in range(2):
        s_scr[s] = first[s]

    def far(j, carry):
        step(j)
        return carry

    lax.fori_loop(0, jnp.maximum(first_near - 1, 0), far, 0)

    for n in range(r + 1):
        @pl.when(first_near + n >= 1)
        def _(n=n):
            step(first_near + n - 1, 1 + n)

    softmax_pv(first_near + r, [s_scr[0], s_scr[1]])

    lam = _lambda(lam_ref, lam_init)
    o = acc_scr[0] / l_scr[0] - lam * (acc_scr[1] / l_scr[1])
    o = o * lax.rsqrt(jnp.mean(o * o, axis=0, keepdims=True) + EPS) * g_ref[...] * (1.0 - lam_init)
    o_ref[...] = o.T.astype(o_ref.dtype)


def _attn_prompt(q, k, v, table, lam_vecs, subln, lam_init, n_heads, tq_pref=1024, tk_pref=512):
    T = q.shape[0]
    w = q.shape[1] // n_heads
    d = w // 2
    tq = _tile(T, tq_pref)
    tk = _tile(tq, tk_pref)
    r = tq // tk
    assert tk % CHUNK == 0 and tk >= REL_MAX_DIST, (tk, "key tiles two back must be at constant bias")
    qt = q.reshape(T // tq, tq, n_heads, w).transpose(2, 0, 3, 1)
    vt = v.reshape(T // tk, tk, n_heads, w).transpose(2, 0, 3, 1)
    near = [_bias_window(table, tk, tq, n * tk, tk) for n in range(r + 1)]
    bias = jnp.stack([jnp.zeros_like(near[0])] + near, axis=1).transpose(0, 1, 3, 2)
    vmem = 2 * T * w * 2 + (r + 2) * tk * tq * 4 + 4 * tq * w * 2 + 2 * tq * w * 4 + 10 * tk * tq * 4
    return pl.pallas_call(
        functools.partial(_attn_prompt_body, tq, tk, d, lam_init),
        grid=(n_heads, T // tq),
        in_specs=[pl.BlockSpec((1, 1, w, tq), lambda h, i: (h, i, 0, 0)),
                  pl.BlockSpec((T, w), lambda h, i: (0, h), pipeline_mode=pl.Buffered(1)),
                  pl.BlockSpec((1, T // tk, w, tk), lambda h, i: (h, 0, 0, 0), pipeline_mode=pl.Buffered(1)),
                  pl.BlockSpec((1, r + 2, tk, tq), lambda h, i: (h, 0, 0, 0), pipeline_mode=pl.Buffered(1)),
                  pl.BlockSpec((4, d), lambda h, i: (0, 0)),
                  pl.BlockSpec((w, 1), lambda h, i: (0, 0))],
        out_specs=pl.BlockSpec((tq, w), lambda h, i: (i, h)),
        out_shape=jax.ShapeDtypeStruct((T, n_heads * w), BF16),
        scratch_shapes=[pltpu.VMEM((2, tk, tq), F32), pltpu.VMEM((2, 1, tq), F32), pltpu.VMEM((2, 1, tq), F32),
                        pltpu.VMEM((2, w, tq), F32)],
        compiler_params=_cparams(2, vmem),
        name="attn_prompt",
    )(qt, k, vt, bias, lam_vecs, subln.reshape(w, 1))


def _attn_sample_body(d, lam_init, q_ref, ck_ref, cv_ref, kn_ref, vn_ref, mb_ref, mbn_ref,
                      lam_ref, g_ref, o_ref, m_scr, l_scr, acc_scr):
    pj = pl.program_id(1)
    last = pl.num_programs(1) - 1
    nt = (((1,), (1,)), ((), ()))

    @pl.when(pj == 0)
    def _():
        m_scr[...] = jnp.full(m_scr.shape, NEG_INF, F32)
        l_scr[...] = jnp.zeros(l_scr.shape, F32)
        acc_scr[...] = jnp.zeros(acc_scr.shape, F32)

    def update(kb, vb, mb):
        for s in range(2):
            g = lax.dot_general(q_ref[0, s], kb[:, s * d:(s + 1) * d], nt, preferred_element_type=F32) + mb
            m_old = m_scr[s]
            m_new = jnp.maximum(m_old, jnp.max(g, axis=-1, keepdims=True))
            alpha = jnp.exp2(m_old - m_new)
            p = jnp.exp2(g - m_new)
            l_scr[s] = alpha * l_scr[s] + jnp.sum(p, axis=-1, keepdims=True)
            acc_scr[s] = alpha * acc_scr[s] + jnp.dot(p.astype(BF16), vb, preferred_element_type=F32)
            m_scr[s] = m_new

    slot = (pj == last).astype(jnp.int32)
    rows = ck_ref.shape[1]
    ch = min(rows, 1024)
    for r0 in range(0, rows, ch):
        update(ck_ref[0, r0:r0 + ch, :].astype(BF16), cv_ref[0, r0:r0 + ch, :].astype(BF16),
               mb_ref[slot, :, r0:r0 + ch])

    @pl.when(pj == last)
    def _():
        update(kn_ref[0], vn_ref[0], mbn_ref[...])
        lam = _lambda(lam_ref, lam_init)
        o = acc_scr[0] / l_scr[0] - lam * (acc_scr[1] / l_scr[1])
        o_ref[0] = (_rms(o, g_ref[...]) * (1.0 - lam_init)).astype(o_ref.dtype)


def _head_mask_bias(bias, n_heads):
    same = jnp.eye(n_heads, dtype=bool)[:, None, None, :]
    full = jnp.where(same, bias[..., None], NEG_INF)
    return full.reshape(bias.shape[0] * bias.shape[1], bias.shape[2] * n_heads)


def _attn_sample(q, kn, vn, cache_k, cache_v, table, lam_vecs, subln, lam_init, n_heads):
    B, S, HW = q.shape
    P = cache_k.shape[1]
    w = HW // n_heads
    d = w // 2
    tp = _tile(P, 256)
    hs = n_heads * S
    assert tp >= REL_MAX_DIST or tp == P, "only the newest cache tile may carry a non-constant bias"
    near = _bias_window(table, P, S, P - tp, tp)
    mb = jnp.stack([_head_mask_bias(jnp.zeros_like(near), n_heads), _head_mask_bias(near, n_heads)])
    mbn = _head_mask_bias(_bias_window(table, P, S, P, S), n_heads)
    qs = q.reshape(B, S, n_heads, 2, d).transpose(0, 3, 2, 1, 4).reshape(B, 2, hs, d)
    new = pl.BlockSpec((1, S * n_heads, w), lambda b, j: (b, 0, 0))
    old = pl.BlockSpec((1, tp * n_heads, w), lambda b, j: (b, j, 0))
    vmem = (2 * 2 * tp * HW * 4 + 2 * tp * HW * 2 + 2 * hs * tp * n_heads * 4 + 4 * hs * tp * n_heads * 4
            + 4 * hs * w * 4)
    o = pl.pallas_call(
        functools.partial(_attn_sample_body, d, lam_init),
        grid=(B, P // tp),
        in_specs=[pl.BlockSpec((1, 2, hs, d), lambda b, j: (b, 0, 0, 0)),
                  old, old, new, new,
                  pl.BlockSpec((2, hs, tp * n_heads), lambda b, j: (0, 0, 0), pipeline_mode=pl.Buffered(1)),
                  pl.BlockSpec((hs, S * n_heads), lambda b, j: (0, 0)),
                  pl.BlockSpec((4, d), lambda b, j: (0, 0)),
                  pl.BlockSpec((1, w), lambda b, j: (0, 0))],
        out_specs=pl.BlockSpec((1, hs, w), lambda b, j: (b, 0, 0)),
        out_shape=jax.ShapeDtypeStruct((B, hs, w), BF16),
        scratch_shapes=[pltpu.VMEM((2, hs, 1), F32), pltpu.VMEM((2, hs, 1), F32), pltpu.VMEM((2, hs, w), F32)],
        compiler_params=_cparams(2, vmem),
        name="attn_sample",
    )(qs, cache_k.reshape(B, P * n_heads, w), cache_v.reshape(B, P * n_heads, w),
      kn.reshape(B, S * n_heads, w), vn.reshape(B, S * n_heads, w), mb, mbn, lam_vecs, subln.reshape(1, w))
    return o.reshape(B, n_heads, S, w).transpose(0, 2, 1, 3).reshape(B, S, HW)


def _conv_body(width, hist_ref, halo_ref, main_ref, w_ref, b_ref, g_ref, beta_ref, o_ref,
               win_scr, sh_scr, conv_scr):
    i = pl.program_id(1)
    tm = main_ref.shape[1]
    C = main_ref.shape[2]

    @pl.when(i == 0)
    def _():
        win_scr[0:CONV_HALO, :] = hist_ref[0]

    @pl.when(i > 0)
    def _():
        win_scr[0:CONV_HALO, :] = halo_ref[0]

    win_scr[CONV_HALO:CONV_HALO + tm, :] = main_ref[0]
    first = CONV_HALO - (width - 1)
    for ph in range(min(8, width)):
        rows = tm + 8 * ((width - 1 - ph) // 8)
        sh_scr[ph, 0:rows, :] = win_scr[first + ph:first + ph + rows, :]
    rc = min(tm, 32)
    cc = min(C, 512)
    for r0 in range(0, tm, rc):
        for c0 in range(0, C, cc):
            acc = jnp.broadcast_to(b_ref[:, c0:c0 + cc], (rc, cc))
            for tap in range(width):
                ph, off = tap % 8, r0 + 8 * (tap // 8)
                acc = acc + sh_scr[ph, off:off + rc, c0:c0 + cc] * w_ref[tap:tap + 1, c0:c0 + cc]
            conv_scr[r0:r0 + rc, c0:c0 + cc] = acc
    c = conv_scr[...]
    xc = c - jnp.mean(c, axis=-1, keepdims=True)
    y = xc * lax.rsqrt(jnp.mean(xc * xc, axis=-1, keepdims=True) + EPS) * g_ref[...] + beta_ref[...]
    o_ref[0] = (y * jax.nn.sigmoid(y)).astype(o_ref.dtype)


def _conv_module(u, hist, w, b, g, beta):
    B, T, C = u.shape
    width = w.shape[0]
    tm = _tile(T, 64)
    nt = T // tm
    assert width - 1 <= CONV_HALO and (tm % CONV_HALO == 0 or nt == 1)
    hist_pad = jnp.concatenate([jnp.zeros((B, CONV_HALO - (width - 1), C), F32), hist], axis=1)
    halo_src = u if nt > 1 else hist_pad
    halo_per_tile = tm // CONV_HALO if nt > 1 else 0
    vec = lambda: pl.BlockSpec((1, C), lambda bb, i: (0, 0))
    sh_rows = tm + 8 * ((width - 1) // 8)
    return pl.pallas_call(
        functools.partial(_conv_body, width),
        grid=(B, nt),
        in_specs=[pl.BlockSpec((1, CONV_HALO, C), lambda bb, i: (bb, 0, 0)),
                  pl.BlockSpec((1, CONV_HALO, C), lambda bb, i: (bb, jnp.maximum(i * halo_per_tile - 1, 0), 0)),
                  pl.BlockSpec((1, tm, C), lambda bb, i: (bb, i, 0)),
                  pl.BlockSpec((width, C), lambda bb, i: (0, 0)),
                  vec(), vec(), vec()],
        out_specs=pl.BlockSpec((1, tm, C), lambda bb, i: (bb, i, 0)),
        out_shape=jax.ShapeDtypeStruct((B, T, C), BF16),
        scratch_shapes=[pltpu.VMEM((CONV_HALO + tm, C), F32), pltpu.VMEM((min(8, width), sh_rows, C), F32),
                        pltpu.VMEM((tm, C), F32)],
        compiler_params=_cparams(2, (16 * (CONV_HALO + tm) + 4 * 8 * sh_rows) * C),
        name="conv_module",
    )(hist_pad, halo_src, u, w, b.reshape(1, C), g.reshape(1, C), beta.reshape(1, C))


def _mem_attn_body(n_heads, dq, dv, q_ref, k_ref, v_ref, o_ref):
    nt = (((1,), (1,)), ((), ()))
    outs = []
    for h in range(n_heads):
        qh = q_ref[0, :, h * dq:(h + 1) * dq]
        kh = k_ref[0, :, h * dq:(h + 1) * dq].astype(BF16)
        vh = v_ref[0, :, h * dv:(h + 1) * dv].astype(BF16)
        s = lax.dot_general(qh, kh, nt, preferred_element_type=F32)
        p = jnp.exp2(s - jnp.max(s, axis=-1, keepdims=True))
        l = jnp.sum(p, axis=-1, keepdims=True)
        outs.append(jnp.dot(p.astype(BF16), vh, preferred_element_type=F32) / l)
    o_ref[0] = jnp.concatenate(outs, axis=-1).astype(o_ref.dtype)


def _mem_attn(mq, mem_k, mem_v, n_heads):
    B, T, QW = mq.shape
    M = mem_k.shape[1]
    VW = mem_v.shape[2]
    tm = _tile(T, 512)
    return pl.pallas_call(
        functools.partial(_mem_attn_body, n_heads, QW // n_heads, VW // n_heads),
        grid=(B, T // tm),
        in_specs=[pl.BlockSpec((1, tm, QW), lambda b, i: (b, i, 0)),
                  pl.BlockSpec((1, M, QW), lambda b, i: (b, 0, 0)),
                  pl.BlockSpec((1, M, VW), lambda b, i: (b, 0, 0))],
        out_specs=pl.BlockSpec((1, tm, VW), lambda b, i: (b, i, 0)),
        out_shape=jax.ShapeDtypeStruct((B, T, VW), BF16),
        compiler_params=_cparams(2, 2 * (tm * QW * 2 + M * (QW + VW) * 4 + tm * VW * 2) + 3 * tm * VW * 4),
        name="mem_attn",
    )(mq, mem_k, mem_v)


def _layer_group(x, past, mem_k, mem_v, table, lam_init, p, dims):
    B, T, D = x.shape
    H, d, MH = dims["H"], dims["d"], dims["MH"]
    R = B * T
    xf = x.reshape(R, D)

    x1, h = _ffn(xf, p["ffn1_norm"], p["ffn1_w_gate"], p["ffn1_w_up"], p["ffn1_w_down"], p["mix_norm"])

    (q,) = _proj(h, [p["w_q"]], functools.partial(_epi_group_rms, d), [BF16], cols=[p["q_gain"]], name="proj_q")
    k32, k16 = _proj(h, [p["w_k"]], functools.partial(_epi_group_rms, d), [F32, BF16], cols=[p["k_gain"]],
                     name="proj_k")
    v32, v16 = _proj(h, [p["w_v"]], _epi_plain, [F32, BF16], name="proj_v")
    (u,) = _proj(h, [p["w_glu_a"], p["w_glu_b"]], _epi_glu, [F32], name="proj_glu")
    (mq,) = _proj(h, [p["w_mq"]], functools.partial(_epi_group_rms, dims["dmq"]), [BF16], cols=[p["mq_gain"]],
                  name="proj_mq")
    (gates,) = _proj(h, [p["w_gates"]], _epi_sigmoid, [BF16], name="proj_gates")

    HW = H * 2 * d
    C = u.shape[1]
    width = p["conv_dw"].shape[0]
    if past is None:
        assert B == 1
        oa = _attn_prompt(q, k16, v16, table, p["lam_vecs"], p["attn_subln"], lam_init, H)
        hist = jnp.zeros((B, width - 1, C), F32)
    else:
        cache_k, cache_v, hist = past
        oa = _attn_sample(q.reshape(B, T, HW), k16.reshape(B, T, HW), v16.reshape(B, T, HW), cache_k, cache_v,
                          table, p["lam_vecs"], p["attn_subln"], lam_init, H).reshape(R, HW)

    u3 = u.reshape(B, T, C)
    conv_state = (u3[:, T - (width - 1):] if T >= width - 1
                  else jnp.concatenate([hist[:, T:], u3], axis=1))
    oc = _conv_module(u3, hist, p["conv_dw"], p["conv_dw_bias"], p["conv_ln_g"], p["conv_ln_b"]).reshape(R, C)

    M = mem_k.shape[1]
    om = _mem_attn(mq.reshape(B, T, -1), mem_k.reshape(B, M, -1), mem_v.reshape(B, M, -1), MH)
    om = om.reshape(R, -1)

    merged = _merge(oa, oc, om, p["w_attn_out"], p["w_conv_out"], p["w_mem_out"], gates)
    (x2,) = _proj(merged, [p["w_out"]], _epi_residual, [F32], tiles=[x1], name="proj_out")
    y = _ffn(x2, p["ffn2_norm"], p["ffn2_w_gate"], p["ffn2_w_up"], p["ffn2_w_down"])
    return (y.reshape(B, T, D), k32.reshape(B, T, H, 2 * d), v32.reshape(B, T, H, 2 * d), conv_state)


def _memory_kv(mem, mem_norm, w_k, w_v, k_gain, dims):
    B, M, D = mem.shape
    hm = _rmsnorm(mem.reshape(B * M, D), mem_norm)
    (k,) = _proj(hm, [w_k], functools.partial(_epi_group_rms, dims["dmq"]), [F32], cols=[k_gain], name="mem_k")
    (v,) = _proj(hm, [w_v], _epi_plain, [F32], name="mem_v")
    MH = dims["MH"]
    return k.reshape(B, M, MH, -1), v.reshape(B, M, MH, -1)


def kernel(x_prompt, x_sample, cache_attn_k, cache_attn_v, cache_conv, cache_mem_k, cache_mem_v, mem_prompt, rel_bias_table, ffn1_norm, ffn1_w_gate, ffn1_w_up, ffn1_w_down, mix_norm, w_in, attn_q_norm, attn_k_norm, lambda_q1, lambda_k1, lambda_q2, lambda_k2, attn_subln, w_attn_out, conv_dw, conv_dw_bias, conv_ln_g, conv_ln_b, w_conv_out, mem_norm, w_mem_kv, mem_q_norm, mem_k_norm, w_mem_out, w_out, ffn2_norm, ffn2_w_gate, ffn2_w_up, ffn2_w_down):
    depth = w_in.shape[0]
    D = x_prompt.shape[-1]
    H = cache_attn_k.shape[3]
    d = attn_q_norm.shape[-1]
    C = conv_dw.shape[-1]
    MH = cache_mem_k.shape[3]
    dmq = cache_mem_k.shape[4]
    dims = {"H": H, "d": d, "MH": MH, "dmq": dmq}
    c_q = H * 2 * d
    c_k, c_v = 2 * c_q, 3 * c_q
    c_conv = c_v + 2 * C
    c_memq = c_conv + MH * dmq
    bf = lambda a: a.astype(BF16)

    yp, ys = x_prompt, x_sample
    outs = [[] for _ in range(8)]
    for l in range(depth):
        lam_init = 0.8 - 0.6 * math.exp(-0.3 * l)
        wi = bf(w_in[l])
        p = {
            "ffn1_norm": ffn1_norm[l], "ffn1_w_gate": bf(ffn1_w_gate[l]), "ffn1_w_up": bf(ffn1_w_up[l]),
            "ffn1_w_down": bf(ffn1_w_down[l]), "mix_norm": mix_norm[l],
            "w_q": (wi, 0, c_q), "w_k": (wi, c_q, c_q), "w_v": (wi, c_k, c_q),
            "w_glu_a": (wi, c_v, C), "w_glu_b": (wi, c_v + C, C),
            "w_mq": (wi, c_conv, MH * dmq), "w_gates": (wi, c_memq, wi.shape[1] - c_memq),
            "q_gain": jnp.tile(attn_q_norm[l] * (d ** -0.5 * LOG2E), 2 * H).reshape(1, c_q),
            "k_gain": jnp.tile(attn_k_norm[l], 2 * H).reshape(1, c_q),
            "mq_gain": jnp.tile(mem_q_norm[l] * (dmq ** -0.5 * LOG2E), MH).reshape(1, MH * dmq),
            "lam_vecs": jnp.stack([lambda_q1[l], lambda_k1[l], lambda_q2[l], lambda_k2[l]]).astype(F32),
            "attn_subln": attn_subln[l], "w_attn_out": bf(w_attn_out[l]),
            "conv_dw": conv_dw[l], "conv_dw_bias": conv_dw_bias[l],
            "conv_ln_g": conv_ln_g[l], "conv_ln_b": conv_ln_b[l], "w_conv_out": bf(w_conv_out[l]),
            "w_mem_out": bf(w_mem_out[l]), "w_out": bf(w_out[l]),
            "ffn2_norm": ffn2_norm[l], "ffn2_w_gate": bf(ffn2_w_gate[l]), "ffn2_w_up": bf(ffn2_w_up[l]),
            "ffn2_w_down": bf(ffn2_w_down[l]),
        }
        wkv = w_mem_kv[l]
        mk, mv = _memory_kv(mem_prompt, mem_norm[l], bf(wkv[:, :MH * dmq]), bf(wkv[:, MH * dmq:]),
                            jnp.tile(mem_k_norm[l], MH).reshape(1, MH * dmq), dims)
        yp, kp, vp, cp = _layer_group(yp, None, mk, mv, rel_bias_table, lam_init, p, dims)
        ys, ks, vs, cs = _layer_group(ys, (cache_attn_k[l], cache_attn_v[l], cache_conv[l]),
                                      cache_mem_k[l], cache_mem_v[l], rel_bias_table, lam_init, p, dims)
        for lst, val in zip(outs, (kp, vp, cp, mk, mv, ks, vs, cs)):
            lst.append(val)
    return (yp, ys) + tuple(jnp.stack(o) for o in outs)
```

```python
import functools
import math

import numpy as np
import jax
import jax.numpy as jnp
from jax import lax
from jax.experimental import pallas as pl
from jax.experimental.pallas import tpu as pltpu

F32 = jnp.float32
BF16 = jnp.bfloat16

EPS = 1e-6
NEG_INF = -1e30
CHUNK = 64
REL_BUCKETS = 32
REL_MAX_DIST = 128
LOG2E = math.log2(math.e)

V7X_VMEM_LIMIT_MAX = 56 * 1024 * 1024
V7X_LANES = 128
V7X_MXU_DIM = 256
CONV_HALO = 32


def _cparams(n_axes, vmem_bytes):
    limit = min(max(int(vmem_bytes * 1.2) + (6 << 20), 32 << 20), V7X_VMEM_LIMIT_MAX)
    return pltpu.CompilerParams(dimension_semantics=("arbitrary",) * n_axes,
                                vmem_limit_bytes=limit)


def _tile(n, pref):
    if n <= pref:
        return n
    t = pref
    while n % t:
        t //= 2
    return t


def _rms(x, g):
    return x * lax.rsqrt(jnp.mean(x * x, axis=-1, keepdims=True) + EPS) * g


def _group_rms(x, g, width):
    parts = []
    for c in range(0, x.shape[-1], width):
        parts.append(_rms(x[:, c:c + width], g[:, c:c + width]))
    return parts[0] if len(parts) == 1 else jnp.concatenate(parts, axis=-1)


def _rmsnorm_body(x_ref, g_ref, o_ref):
    o_ref[...] = _rms(x_ref[...], g_ref[...]).astype(o_ref.dtype)


def _rmsnorm(x, g):
    R, D = x.shape
    tm = _tile(R, 256)
    return pl.pallas_call(
        _rmsnorm_body,
        grid=(R // tm,),
        in_specs=[pl.BlockSpec((tm, D), lambda i: (i, 0)),
                  pl.BlockSpec((1, D), lambda i: (0, 0))],
        out_specs=pl.BlockSpec((tm, D), lambda i: (i, 0)),
        out_shape=jax.ShapeDtypeStruct((R, D), BF16),
        compiler_params=_cparams(1, 2 * tm * D * 6),
        name="rmsnorm",
    )(x, g.reshape(1, D))


def _ffn_body(emit_norm, nj, x_ref, g_ref, wg_ref, wu_ref, wd_ref, *rest):
    if emit_norm:
        g2_ref, out_ref, hn_ref, xn_scr = rest
    else:
        out_ref, xn_scr = rest
    j = pl.program_id(1)

    tm = out_ref.shape[0]
    rc = min(tm, 64)

    @pl.when(j == 0)
    def _():
        for r0 in range(0, tm, rc):
            x = x_ref[r0:r0 + rc, :]
            xn_scr[r0:r0 + rc, :] = _rms(x, g_ref[...]).astype(BF16)
            out_ref[r0:r0 + rc, :] = x

    xn = xn_scr[...]
    a = jnp.dot(xn, wg_ref[...], preferred_element_type=F32)
    b = jnp.dot(xn, wu_ref[...], preferred_element_type=F32)
    h = (a * jax.nn.sigmoid(a) * b * 0.5).astype(BF16)
    D = out_ref.shape[1]
    cw = min(D, 1024)
    for c0 in range(0, D, cw):
        out_ref[:, c0:c0 + cw] += jnp.dot(h, wd_ref[:, c0:c0 + cw], preferred_element_type=F32)

    if emit_norm:
        @pl.when(j == nj - 1)
        def _():
            for r0 in range(0, tm, rc):
                hn_ref[r0:r0 + rc, :] = _rms(out_ref[r0:r0 + rc, :], g2_ref[...]).astype(BF16)


def _ffn(x, g, wg, wu, wd, g2=None):
    R, D = x.shape
    Fdim = wg.shape[1]
    tm = _tile(R, 512)
    tf = _tile(Fdim, V7X_MXU_DIM)
    nj = Fdim // tf
    emit_norm = g2 is not None
    in_specs = [pl.BlockSpec((tm, D), lambda i, j: (i, 0), pipeline_mode=pl.Buffered(1)),
                pl.BlockSpec((1, D), lambda i, j: (0, 0)),
                pl.BlockSpec((D, tf), lambda i, j: (0, j)),
                pl.BlockSpec((D, tf), lambda i, j: (0, j)),
                pl.BlockSpec((tf, D), lambda i, j: (j, 0))]
    args = [x, g.reshape(1, D), wg, wu, wd]
    out_specs = [pl.BlockSpec((tm, D), lambda i, j: (i, 0))]
    out_shape = [jax.ShapeDtypeStruct((R, D), F32)]
    vmem = tm * D * 4 + 2 * tm * D * 4 + tm * D * 2 + 2 * 3 * D * tf * 2 + 4 * tm * tf * 4
    if emit_norm:
        in_specs.append(pl.BlockSpec((1, D), lambda i, j: (0, 0)))
        args.append(g2.reshape(1, D))
        out_specs.append(pl.BlockSpec((tm, D), lambda i, j: (i, 0)))
        out_shape.append(jax.ShapeDtypeStruct((R, D), BF16))
        vmem += 2 * tm * D * 2
    res = pl.pallas_call(
        functools.partial(_ffn_body, emit_norm, nj),
        grid=(R // tm, nj),
        in_specs=in_specs,
        out_specs=out_specs,
        out_shape=out_shape,
        scratch_shapes=[pltpu.VMEM((tm, D), BF16)],
        compiler_params=_cparams(2, vmem),
        name="ffn_norm" if emit_norm else "ffn",
    )(*args)
    return res if emit_norm else res[0]


def _proj_body(epi, n_w, n_ex, a_ref, *refs):
    w_refs = refs[:n_w]
    ex_refs = refs[n_w:n_w + n_ex]
    out_refs = refs[n_w + n_ex:]
    a = a_ref[...]
    accs = [jnp.dot(a, w[...], preferred_element_type=F32) for w in w_refs]
    y = epi(accs, [e[...] for e in ex_refs])
    for o in out_refs:
        o[...] = y.astype(o.dtype)


def _proj(a, ws, epi, out_dtypes, *, cols=(), tiles=(), name, tm_pref=1024, tn_pref=512):
    R, K = a.shape
    ws = [w if isinstance(w, tuple) else (w, 0, w.shape[1]) for w in ws]
    N = ws[0][2]
    tm = _tile(R, tm_pref)
    tn = _tile(N, tn_pref)
    ws = [(w, c0 // tn) if c0 % tn == 0 else (w[:, c0:c0 + N], 0) for w, c0, _ in ws]
    in_specs = [pl.BlockSpec((tm, K), lambda i, j: (i, 0))]
    in_specs += [pl.BlockSpec((K, tn), lambda i, j, j0=j0: (0, j0 + j)) for _, j0 in ws]
    in_specs += [pl.BlockSpec((1, tn), lambda i, j: (0, j)) for _ in cols]
    in_specs += [pl.BlockSpec((tm, tn), lambda i, j: (i, j)) for _ in tiles]
    vmem = 2 * tm * K * 2 + 2 * len(ws) * K * tn * 2 + (len(ws) + 2) * tm * tn * 4
    vmem += sum(2 * tm * tn * t.dtype.itemsize for t in tiles)
    vmem += sum(2 * tm * tn * jnp.dtype(d).itemsize for d in out_dtypes)
    res = pl.pallas_call(
        functools.partial(_proj_body, epi, len(ws), len(cols) + len(tiles)),
        grid=(R // tm, N // tn),
        in_specs=in_specs,
        out_specs=[pl.BlockSpec((tm, tn), lambda i, j: (i, j)) for _ in out_dtypes],
        out_shape=[jax.ShapeDtypeStruct((R, N), d) for d in out_dtypes],
        compiler_params=_cparams(2, vmem),
        name=name,
    )(a, *[w for w, _ in ws], *cols, *tiles)
    return res


def _epi_plain(accs, ex):
    return accs[0]


def _epi_group_rms(width, accs, ex):
    return _group_rms(accs[0], ex[0], width)


def _epi_glu(accs, ex):
    return accs[0] * jax.nn.sigmoid(accs[1])


def _epi_sigmoid(accs, ex):
    return jax.nn.sigmoid(accs[0])


def _epi_residual(accs, ex):
    return ex[0] + accs[0]


def _merge_body(a_ref, c_ref, m_ref, wa_ref, wc_ref, wm_ref, ga_ref, gc_ref, gm_ref, o_ref):
    ya = jnp.dot(a_ref[...], wa_ref[...], preferred_element_type=F32)
    yc = jnp.dot(c_ref[...], wc_ref[...], preferred_element_type=F32)
    ym = jnp.dot(m_ref[...], wm_ref[...], preferred_element_type=F32)
    merged = ga_ref[...].astype(F32) * ya + gc_ref[...].astype(F32) * yc + gm_ref[...].astype(F32) * ym
    o_ref[...] = merged.astype(o_ref.dtype)


def _merge(oa, oc, om, wa, wc, wm, gates):
    R = oa.shape[0]
    D = wa.shape[1]
    tm = _tile(R, 512)
    tn = _tile(D, 512)
    nt = D // tn
    row = lambda arr: pl.BlockSpec((tm, arr.shape[1]), lambda i, j: (i, 0))
    col = lambda arr: pl.BlockSpec((arr.shape[0], tn), lambda i, j: (0, j))
    gate = lambda b: pl.BlockSpec((tm, tn), lambda i, j: (i, b * nt + j))
    ksum = oa.shape[1] + oc.shape[1] + om.shape[1]
    vmem = 2 * tm * ksum * 2 + 2 * ksum * tn * 2 + 2 * 4 * tm * tn * 2 + 4 * tm * tn * 4
    return pl.pallas_call(
        _merge_body,
        grid=(R // tm, nt),
        in_specs=[row(oa), row(oc), row(om), col(wa), col(wc), col(wm), gate(0), gate(1), gate(2)],
        out_specs=pl.BlockSpec((tm, tn), lambda i, j: (i, j)),
        out_shape=jax.ShapeDtypeStruct((R, D), BF16),
        compiler_params=_cparams(2, vmem),
        name="merge",
    )(oa, oc, om, wa, wc, wm, gates, gates, gates)


def _t5_bucket(rel):
    nb = REL_BUCKETS // 2
    max_exact = nb // 2
    n = jnp.abs(rel)
    nf = jnp.maximum(n, max_exact).astype(F32)
    large = max_exact + (jnp.log(nf / max_exact) / math.log(REL_MAX_DIST / max_exact)
                         * (nb - max_exact)).astype(jnp.int32)
    large = jnp.minimum(large, nb - 1)
    return jnp.where(rel > 0, nb, 0) + jnp.where(n < max_exact, n, large)


def _bias_window(table, q0, nq, k0, nk):
    far = table[REL_BUCKETS // 2 - 1].astype(F32)

    def toeplitz(off, n, m):
        span = n + m - 1
        rel = off - (n - 1) + np.arange(span, dtype=np.int32)
        v = ((table[_t5_bucket(jnp.asarray(rel))].astype(F32) - far) * LOG2E).T
        rows = jnp.tile(jnp.pad(v, ((0, 0), (0, 1))), (1, n))[:, :n * span].reshape(v.shape[0], n, span)
        return rows[:, :, n - 1:n - 1 + m]

    g = REL_MAX_DIST
    if nq % g or nk % g or (k0 - q0) % g:
        b = toeplitz(k0 - q0, nq, nk)
    else:
        a_idx = np.arange(nq // g)[:, None]
        b_idx = np.arange(nk // g)[None, :]
        diag = (k0 - q0) // g + b_idx - a_idx
        b = 0.0
        for c in (-1, 0, 1):
            place = jnp.asarray((diag == c).astype(np.float32))
            b = b + place[None, :, None, :, None] * toeplitz(c * g, g, g)[:, None, :, None, :]
        b = b.reshape(b.shape[0], nq, nk)
    qpos = q0 + np.arange(nq)
    kpos = k0 + np.arange(nk)
    visible = (kpos[None, :] // CHUNK) <= (qpos[:, None] // CHUNK)
    return jnp.where(jnp.asarray(visible)[None], b, NEG_INF)


def _lambda(lam_ref, lam_init):
    lv = lam_ref[...]
    s1 = jnp.sum(lv[0:1] * lv[1:2], axis=-1, keepdims=True)
    s2 = jnp.sum(lv[2:3] * lv[3:4], axis=-1, keepdims=True)
    return jnp.exp(s1) - jnp.exp(s2) + lam_init


def _attn_prompt_body(tq, tk, d, lam_init, qt_ref, k_ref, vt_ref, bias_ref, lam_ref, g_ref, o_ref,
                      s_scr, m_scr, l_scr, acc_scr):
    i = pl.program_id(1)
    r = tq // tk
    first_near = r * i - 1
    m_scr[...] = jnp.full(m_scr.shape, NEG_INF, F32)
    l_scr[...] = jnp.zeros(l_scr.shape, F32)
    acc_scr[...] = jnp.zeros(acc_scr.shape, F32)

    def scores(j, half, slot=None):
        row0 = pl.multiple_of(j * tk, tk)
        for s in range(2):
            sc = jnp.dot(k_ref[pl.ds(row0, tk), s * d:(s + 1) * d], qt_ref[0, 0, s * d:(s + 1) * d, :],
                         preferred_element_type=F32)
            if slot is not None:
                sc = sc + bias_ref[0, slot]
            s_scr[half, s] = sc

    def softmax_pv(j, half):
        vt = vt_ref[0, j]
        for s in range(2):
            sc = s_scr[half, s]
            m_old = m_scr[s]
            m_new = jnp.maximum(m_old, jnp.max(sc, axis=0, keepdims=True))
            alpha = jnp.exp2(m_old - m_new)
            p = jnp.exp2(sc - m_new)
            l_scr[s] = alpha * l_scr[s] + jnp.sum(p, axis=0, keepdims=True)
            acc_scr[s] = alpha * acc_scr[s] + jnp.dot(vt, p.astype(BF16), preferred_element_type=F32)
            m_scr[s] = m_new

    def step(j, half, slot=None):
        scores(j + 1, 1 - half, slot)
        softmax_pv(j, half)

    scores(0, 0, jnp.clip(1 - first_near, 0, r + 1))
    n_far = jnp.maximum(first_near - 1, 0)

    if r % 2 == 0:
        def far2(jp, carry):
            step(2 * jp, 0)
            step(2 * jp + 1, 1)
            return carry

        lax.fori_loop(0, n_far // 2, far2, 0)
        half_of = lambda n: n % 2
    else:
        def far1(j, carry):
            step(j, j % 2)
            return carry

        lax.fori_loop(0, n_far, far1, 0)
        half_of = lambda n: (first_near + n - 1) % 2

    for n in range(r + 1):
        @pl.when(first_near + n >= 1)
        def _(n=n):
            step(first_near + n - 1, half_of(n), 1 + n)

    softmax_pv(first_near + r, half_of(r + 1))

    lam = _lambda(lam_ref, lam_init)
    o = acc_scr[0] / l_scr[0] - lam * (acc_scr[1] / l_scr[1])
    o = o * lax.rsqrt(jnp.mean(o * o, axis=0, keepdims=True) + EPS) * g_ref[...] * (1.0 - lam_init)
    o_ref[...] = o.T.astype(o_ref.dtype)


def _attn_prompt(q, k, v, table, lam_vecs, subln, lam_init, n_heads, tq_pref=1024, tk_pref=512):
    T = q.shape[0]
    w = q.shape[1] // n_heads
    d = w // 2
    tq = _tile(T, tq_pref)
    tk = _tile(tq, tk_pref)
    r = tq // tk
    assert tk % CHUNK == 0 and tk >= REL_MAX_DIST, (tk, "key tiles two back must be at constant bias")
    qt = q.reshape(T // tq, tq, n_heads, w).transpose(2, 0, 3, 1)
    vt = v.reshape(T // tk, tk, n_heads, w).transpose(2, 0, 3, 1)
    near = [_bias_window(table, tk, tq, n * tk, tk) for n in range(r + 1)]
    bias = jnp.stack([jnp.zeros_like(near[0])] + near, axis=1).transpose(0, 1, 3, 2)
    vmem = 2 * T * w * 2 + (r + 2) * tk * tq * 4 + 4 * tq * w * 2 + 2 * tq * w * 4 + 10 * tk * tq * 4
    return pl.pallas_call(
        functools.partial(_attn_prompt_body, tq, tk, d, lam_init),
        grid=(n_heads, T // tq),
        in_specs=[pl.BlockSpec((1, 1, w, tq), lambda h, i: (h, i, 0, 0)),
                  pl.BlockSpec((T, w), lambda h, i: (0, h), pipeline_mode=pl.Buffered(1)),
                  pl.BlockSpec((1, T // tk, w, tk), lambda h, i: (h, 0, 0, 0), pipeline_mode=pl.Buffered(1)),
                  pl.BlockSpec((1, r + 2, tk, tq), lambda h, i: (h, 0, 0, 0), pipeline_mode=pl.Buffered(1)),
                  pl.BlockSpec((4, d), lambda h, i: (0, 0)),
                  pl.BlockSpec((w, 1), lambda h, i: (0, 0))],
        out_specs=pl.BlockSpec((tq, w), lambda h, i: (i, h)),
        out_shape=jax.ShapeDtypeStruct((T, n_heads * w), BF16),
        scratch_shapes=[pltpu.VMEM((2, 2, tk, tq), F32), pltpu.VMEM((2, 1, tq), F32),
                        pltpu.VMEM((2, 1, tq), F32), pltpu.VMEM((2, w, tq), F32)],
        compiler_params=_cparams(2, vmem),
        name="attn_prompt",
    )(qt, k, vt, bias, lam_vecs, subln.reshape(w, 1))


def _attn_sample_body(d, lam_init, q_ref, ck_ref, cv_ref, kn_ref, vn_ref, mb_ref, mbn_ref,
                      lam_ref, g_ref, o_ref, m_scr, l_scr, acc_scr):
    pj = pl.program_id(1)
    last = pl.num_programs(1) - 1
    nt = (((1,), (1,)), ((), ()))

    @pl.when(pj == 0)
    def _():
        m_scr[...] = jnp.full(m_scr.shape, NEG_INF, F32)
        l_scr[...] = jnp.zeros(l_scr.shape, F32)
        acc_scr[...] = jnp.zeros(acc_scr.shape, F32)

    def update(kb, vb, mb):
        for s in range(2):
            g = lax.dot_general(q_ref[0, s], kb[:, s * d:(s + 1) * d], nt, preferred_element_type=F32) + mb
            m_old = m_scr[s]
            m_new = jnp.maximum(m_old, jnp.max(g, axis=-1, keepdims=True))
            alpha = jnp.exp2(m_old - m_new)
            p = jnp.exp2(g - m_new)
            l_scr[s] = alpha * l_scr[s] + jnp.sum(p, axis=-1, keepdims=True)
            acc_scr[s] = alpha * acc_scr[s] + jnp.dot(p.astype(BF16), vb, preferred_element_type=F32)
            m_scr[s] = m_new

    slot = (pj == last).astype(jnp.int32)
    rows = ck_ref.shape[1]
    ch = min(rows, 1024)
    for r0 in range(0, rows, ch):
        update(ck_ref[0, r0:r0 + ch, :].astype(BF16), cv_ref[0, r0:r0 + ch, :].astype(BF16),
               mb_ref[slot, :, r0:r0 + ch])

    @pl.when(pj == last)
    def _():
        update(kn_ref[0], vn_ref[0], mbn_ref[...])
        lam = _lambda(lam_ref, lam_init)
        o = acc_scr[0] / l_scr[0] - lam * (acc_scr[1] / l_scr[1])
        o_ref[0] = (_rms(o, g_ref[...]) * (1.0 - lam_init)).astype(o_ref.dtype)


def _head_mask_bias(bias, n_heads):
    same = jnp.eye(n_heads, dtype=bool)[:, None, None, :]
    full = jnp.where(same, bias[..., None], NEG_INF)
    return full.reshape(bias.shape[0] * bias.shape[1], bias.shape[2] * n_heads)


def _attn_sample(q, kn, vn, cache_k, cache_v, table, lam_vecs, subln, lam_init, n_heads):
    B, S, HW = q.shape
    P = cache_k.shape[1]
    w = HW // n_heads
    d = w // 2
    tp = _tile(P, 256)
    hs = n_heads * S
    assert tp >= REL_MAX_DIST or tp == P, "only the newest cache tile may carry a non-constant bias"
    near = _bias_window(table, P, S, P - tp, tp)
    mb = jnp.stack([_head_mask_bias(jnp.zeros_like(near), n_heads), _head_mask_bias(near, n_heads)])
    mbn = _head_mask_bias(_bias_window(table, P, S, P, S), n_heads)
    qs = q.reshape(B, S, n_heads, 2, d).transpose(0, 3, 2, 1, 4).reshape(B, 2, hs, d)
    new = pl.BlockSpec((1, S * n_heads, w), lambda b, j: (b, 0, 0))
    old = pl.BlockSpec((1, tp * n_heads, w), lambda b, j: (b, j, 0))
    vmem = (2 * 2 * tp * HW * 4 + 2 * tp * HW * 2 + 2 * hs * tp * n_heads * 4 + 4 * hs * tp * n_heads * 4
            + 4 * hs * w * 4)
    o = pl.pallas_call(
        functools.partial(_attn_sample_body, d, lam_init),
        grid=(B, P // tp),
        in_specs=[pl.BlockSpec((1, 2, hs, d), lambda b, j: (b, 0, 0, 0)),
                  old, old, new, new,
                  pl.BlockSpec((2, hs, tp * n_heads), lambda b, j: (0, 0, 0), pipeline_mode=pl.Buffered(1)),
                  pl.BlockSpec((hs, S * n_heads), lambda b, j: (0, 0)),
                  pl.BlockSpec((4, d), lambda b, j: (0, 0)),
                  pl.BlockSpec((1, w), lambda b, j: (0, 0))],
        out_specs=pl.BlockSpec((1, hs, w), lambda b, j: (b, 0, 0)),
        out_shape=jax.ShapeDtypeStruct((B, hs, w), BF16),
        scratch_shapes=[pltpu.VMEM((2, hs, 1), F32), pltpu.VMEM((2, hs, 1), F32), pltpu.VMEM((2, hs, w), F32)],
        compiler_params=_cparams(2, vmem),
        name="attn_sample",
    )(qs, cache_k.reshape(B, P * n_heads, w), cache_v.reshape(B, P * n_heads, w),
      kn.reshape(B, S * n_heads, w), vn.reshape(B, S * n_heads, w), mb, mbn, lam_vecs, subln.reshape(1, w))
    return o.reshape(B, n_heads, S, w).transpose(0, 2, 1, 3).reshape(B, S, HW)


def _conv_body(width, hist_ref, halo_ref, main_ref, w_ref, b_ref, g_ref, beta_ref, o_ref,
               win_scr, sh_scr, conv_scr):
    i = pl.program_id(1)
    tm = main_ref.shape[1]
    C = main_ref.shape[2]

    @pl.when(i == 0)
    def _():
        win_scr[0:CONV_HALO, :] = hist_ref[0]

    @pl.when(i > 0)
    def _():
        win_scr[0:CONV_HALO, :] = halo_ref[0]

    win_scr[CONV_HALO:CONV_HALO + tm, :] = main_ref[0]
    first = CONV_HALO - (width - 1)
    for ph in range(min(8, width)):
        rows = tm + 8 * ((width - 1 - ph) // 8)
        sh_scr[ph, 0:rows, :] = win_scr[first + ph:first + ph + rows, :]
    rc = min(tm, 32)
    cc = min(C, 512)
    for r0 in range(0, tm, rc):
        for c0 in range(0, C, cc):
            acc = jnp.broadcast_to(b_ref[:, c0:c0 + cc], (rc, cc))
            for tap in range(width):
                ph, off = tap % 8, r0 + 8 * (tap // 8)
                acc = acc + sh_scr[ph, off:off + rc, c0:c0 + cc] * w_ref[tap:tap + 1, c0:c0 + cc]
            conv_scr[r0:r0 + rc, c0:c0 + cc] = acc
    c = conv_scr[...]
    xc = c - jnp.mean(c, axis=-1, keepdims=True)
    y = xc * lax.rsqrt(jnp.mean(xc * xc, axis=-1, keepdims=True) + EPS) * g_ref[...] + beta_ref[...]
    o_ref[0] = (y * jax.nn.sigmoid(y)).astype(o_ref.dtype)


def _conv_module(u, hist, w, b, g, beta):
    B, T, C = u.shape
    width = w.shape[0]
    tm = _tile(T, 64)
    nt = T // tm
    assert width - 1 <= CONV_HALO and (tm % CONV_HALO == 0 or nt == 1)
    hist_pad = jnp.concatenate([jnp.zeros((B, CONV_HALO - (width - 1), C), F32), hist], axis=1)
    halo_src = u if nt > 1 else hist_pad
    halo_per_tile = tm // CONV_HALO if nt > 1 else 0
    vec = lambda: pl.BlockSpec((1, C), lambda bb, i: (0, 0))
    sh_rows = tm + 8 * ((width - 1) // 8)
    return pl.pallas_call(
        functools.partial(_conv_body, width),
        grid=(B, nt),
        in_specs=[pl.BlockSpec((1, CONV_HALO, C), lambda bb, i: (bb, 0, 0)),
                  pl.BlockSpec((1, CONV_HALO, C), lambda bb, i: (bb, jnp.maximum(i * halo_per_tile - 1, 0), 0)),
                  pl.BlockSpec((1, tm, C), lambda bb, i: (bb, i, 0)),
                  pl.BlockSpec((width, C), lambda bb, i: (0, 0)),
                  vec(), vec(), vec()],
        out_specs=pl.BlockSpec((1, tm, C), lambda bb, i: (bb, i, 0)),
        out_shape=jax.ShapeDtypeStruct((B, T, C), BF16),
        scratch_shapes=[pltpu.VMEM((CONV_HALO + tm, C), F32), pltpu.VMEM((min(8, width), sh_rows, C), F32),
                        pltpu.VMEM((tm, C), F32)],
        compiler_params=_cparams(2, (16 * (CONV_HALO + tm) + 4 * 8 * sh_rows) * C),
        name="conv_module",
    )(hist_pad, halo_src, u, w, b.reshape(1, C), g.reshape(1, C), beta.reshape(1, C))


def _mem_attn_body(n_heads, dq, dv, q_ref, k_ref, v_ref, o_ref):
    nt = (((1,), (1,)), ((), ()))
    outs = []
    for h in range(n_heads):
        qh = q_ref[0, :, h * dq:(h + 1) * dq]
        kh = k_ref[0, :, h * dq:(h + 1) * dq].astype(BF16)
        vh = v_ref[0, :, h * dv:(h + 1) * dv].astype(BF16)
        s = lax.dot_general(qh, kh, nt, preferred_element_type=F32)
        p = jnp.exp2(s - jnp.max(s, axis=-1, keepdims=True))
        l = jnp.sum(p, axis=-1, keepdims=True)
        outs.append(jnp.dot(p.astype(BF16), vh, preferred_element_type=F32) / l)
    o_ref[0] = jnp.concatenate(outs, axis=-1).astype(o_ref.dtype)


def _mem_attn(mq, mem_k, mem_v, n_heads):
    B, T, QW = mq.shape
    M = mem_k.shape[1]
    VW = mem_v.shape[2]
    tm = _tile(T, 512)
    return pl.pallas_call(
        functools.partial(_mem_attn_body, n_heads, QW // n_heads, VW // n_heads),
        grid=(B, T // tm),
        in_specs=[pl.BlockSpec((1, tm, QW), lambda b, i: (b, i, 0)),
                  pl.BlockSpec((1, M, QW), lambda b, i: (b, 0, 0)),
                  pl.BlockSpec((1, M, VW), lambda b, i: (b, 0, 0))],
        out_specs=pl.BlockSpec((1, tm, VW), lambda b, i: (b, i, 0)),
        out_shape=jax.ShapeDtypeStruct((B, T, VW), BF16),
        compiler_params=_cparams(2, 2 * (tm * QW * 2 + M * (QW + VW) * 4 + tm * VW * 2) + 3 * tm * VW * 4),
        name="mem_attn",
    )(mq, mem_k, mem_v)


def _layer_group(x, past, mem_k, mem_v, table, lam_init, p, dims):
    B, T, D = x.shape
    H, d, MH = dims["H"], dims["d"], dims["MH"]
    R = B * T
    xf = x.reshape(R, D)

    x1, h = _ffn(xf, p["ffn1_norm"], p["ffn1_w_gate"], p["ffn1_w_up"], p["ffn1_w_down"], p["mix_norm"])

    (q,) = _proj(h, [p["w_q"]], functools.partial(_epi_group_rms, d), [BF16], cols=[p["q_gain"]], name="proj_q")
    k32, k16 = _proj(h, [p["w_k"]], functools.partial(_epi_group_rms, d), [F32, BF16], cols=[p["k_gain"]],
                     name="proj_k")
    v32, v16 = _proj(h, [p["w_v"]], _epi_plain, [F32, BF16], name="proj_v")
    (u,) = _proj(h, [p["w_glu_a"], p["w_glu_b"]], _epi_glu, [F32], name="proj_glu")
    (mq,) = _proj(h, [p["w_mq"]], functools.partial(_epi_group_rms, dims["dmq"]), [BF16], cols=[p["mq_gain"]],
                  name="proj_mq")
    (gates,) = _proj(h, [p["w_gates"]], _epi_sigmoid, [BF16], name="proj_gates")

    HW = H * 2 * d
    C = u.shape[1]
    width = p["conv_dw"].shape[0]
    if past is None:
        assert B == 1
        oa = _attn_prompt(q, k16, v16, table, p["lam_vecs"], p["attn_subln"], lam_init, H)
        hist = jnp.zeros((B, width - 1, C), F32)
    else:
        cache_k, cache_v, hist = past
        oa = _attn_sample(q.reshape(B, T, HW), k16.reshape(B, T, HW), v16.reshape(B, T, HW), cache_k, cache_v,
                          table, p["lam_vecs"], p["attn_subln"], lam_init, H).reshape(R, HW)

    u3 = u.reshape(B, T, C)
    conv_state = (u3[:, T - (width - 1):] if T >= width - 1
                  else jnp.concatenate([hist[:, T:], u3], axis=1))
    oc = _conv_module(u3, hist, p["conv_dw"], p["conv_dw_bias"], p["conv_ln_g"], p["conv_ln_b"]).reshape(R, C)

    M = mem_k.shape[1]
    om = _mem_attn(mq.reshape(B, T, -1), mem_k.reshape(B, M, -1), mem_v.reshape(B, M, -1), MH)
    om = om.reshape(R, -1)

    merged = _merge(oa, oc, om, p["w_attn_out"], p["w_conv_out"], p["w_mem_out"], gates)
    (x2,) = _proj(merged, [p["w_out"]], _epi_residual, [F32], tiles=[x1], name="proj_out")
    y = _ffn(x2, p["ffn2_norm"], p["ffn2_w_gate"], p["ffn2_w_up"], p["ffn2_w_down"])
    return (y.reshape(B, T, D), k32.reshape(B, T, H, 2 * d), v32.reshape(B, T, H, 2 * d), conv_state)


def _memory_kv(mem, mem_norm, w_k, w_v, k_gain, dims):
    B, M, D = mem.shape
    hm = _rmsnorm(mem.reshape(B * M, D), mem_norm)
    (k,) = _proj(hm, [w_k], functools.partial(_epi_group_rms, dims["dmq"]), [F32], cols=[k_gain], name="mem_k")
    (v,) = _proj(hm, [w_v], _epi_plain, [F32], name="mem_v")
    MH = dims["MH"]
    return k.reshape(B, M, MH, -1), v.reshape(B, M, MH, -1)


def kernel(x_prompt, x_sample, cache_attn_k, cache_attn_v, cache_conv, cache_mem_k, cache_mem_v, mem_prompt, rel_bias_table, ffn1_norm, ffn1_w_gate, ffn1_w_up, ffn1_w_down, mix_norm, w_in, attn_q_norm, attn_k_norm, lambda_q1, lambda_k1, lambda_q2, lambda_k2, attn_subln, w_attn_out, conv_dw, conv_dw_bias, conv_ln_g, conv_ln_b, w_conv_out, mem_norm, w_mem_kv, mem_q_norm, mem_k_norm, w_mem_out, w_out, ffn2_norm, ffn2_w_gate, ffn2_w_up, ffn2_w_down):
    depth = w_in.shape[0]
    D = x_prompt.shape[-1]
    H = cache_attn_k.shape[3]
    d = attn_q_norm.shape[-1]
    C = conv_dw.shape[-1]
    MH = cache_mem_k.shape[3]
    dmq = cache_mem_k.shape[4]
    dims = {"H": H, "d": d, "MH": MH, "dmq": dmq}
    c_q = H * 2 * d
    c_k, c_v = 2 * c_q, 3 * c_q
    c_conv = c_v + 2 * C
    c_memq = c_conv + MH * dmq
    bf = lambda a: a.astype(BF16)

    yp, ys = x_prompt, x_sample
    outs = [[] for _ in range(8)]
    for l in range(depth):
        lam_init = 0.8 - 0.6 * math.exp(-0.3 * l)
        wi = bf(w_in[l])
        p = {
            "ffn1_norm": ffn1_norm[l], "ffn1_w_gate": bf(ffn1_w_gate[l]), "ffn1_w_up": bf(ffn1_w_up[l]),
            "ffn1_w_down": bf(ffn1_w_down[l]), "mix_norm": mix_norm[l],
            "w_q": (wi, 0, c_q), "w_k": (wi, c_q, c_q), "w_v": (wi, c_k, c_q),
            "w_glu_a": (wi, c_v, C), "w_glu_b": (wi, c_v + C, C),
            "w_mq": (wi, c_conv, MH * dmq), "w_gates": (wi, c_memq, wi.shape[1] - c_memq),
            "q_gain": jnp.tile(attn_q_norm[l] * (d ** -0.5 * LOG2E), 2 * H).reshape(1, c_q),
            "k_gain": jnp.tile(attn_k_norm[l], 2 * H).reshape(1, c_q),
            "mq_gain": jnp.tile(mem_q_norm[l] * (dmq ** -0.5 * LOG2E), MH).reshape(1, MH * dmq),
            "lam_vecs": jnp.stack([lambda_q1[l], lambda_k1[l], lambda_q2[l], lambda_k2[l]]).astype(F32),
            "attn_subln": attn_subln[l], "w_attn_out": bf(w_attn_out[l]),
            "conv_dw": conv_dw[l], "conv_dw_bias": conv_dw_bias[l],
            "conv_ln_g": conv_ln_g[l], "conv_ln_b": conv_ln_b[l], "w_conv_out": bf(w_conv_out[l]),
            "w_mem_out": bf(w_mem_out[l]), "w_out": bf(w_out[l]),
            "ffn2_norm": ffn2_norm[l], "ffn2_w_gate": bf(ffn2_w_gate[l]), "ffn2_w_up": bf(ffn2_w_up[l]),
            "ffn2_w_down": bf(ffn2_w_down[l]),
        }
        wkv = w_mem_kv[l]
        mk, mv = _memory_kv(mem_prompt, mem_norm[l], bf(wkv[:, :MH * dmq]), bf(wkv[:, MH * dmq:]),
                            jnp.tile(mem_k_norm[l], MH).reshape(1, MH * dmq), dims)
        yp, kp, vp, cp = _layer_group(yp, None, mk, mv, rel_bias_table, lam_init, p, dims)
        ys, ks, vs, cs = _layer_group(ys, (cache_attn_k[l], cache_attn_v[l], cache_conv[l]),
                                      cache_mem_k[l], cache_mem_v[l], rel_bias_table, lam_init, p, dims)
        for lst, val in zip(outs, (kp, vp, cp, mk, mv, ks, vs, cs)):
            lst.append(val)
    return (yp, ys) + tuple(jnp.stack(o) for o in outs)
```

```python
import functools
import math

import numpy as np
import jax
import jax.numpy as jnp
from jax import lax
from jax.experimental import pallas as pl
from jax.experimental.pallas import tpu as pltpu

F32 = jnp.float32
BF16 = jnp.bfloat16

EPS = 1e-6
NEG_INF = -1e30
CHUNK = 64
REL_BUCKETS = 32
REL_MAX_DIST = 128
LOG2E = math.log2(math.e)

V7X_VMEM_LIMIT_MAX = 56 * 1024 * 1024
V7X_LANES = 128
V7X_MXU_DIM = 256
CONV_HALO = 32


def _cparams(n_axes, vmem_bytes):
    limit = min(max(int(vmem_bytes * 1.2) + (6 << 20), 32 << 20), V7X_VMEM_LIMIT_MAX)
    return pltpu.CompilerParams(dimension_semantics=("arbitrary",) * n_axes,
                                vmem_limit_bytes=limit)


def _tile(n, pref):
    if n <= pref:
        return n
    t = pref
    while n % t:
        t //= 2
    return t


def _rms(x, g):
    return x * lax.rsqrt(jnp.mean(x * x, axis=-1, keepdims=True) + EPS) * g


def _group_rms(x, g, width):
    parts = []
    for c in range(0, x.shape[-1], width):
        parts.append(_rms(x[:, c:c + width], g[:, c:c + width]))
    return parts[0] if len(parts) == 1 else jnp.concatenate(parts, axis=-1)


def _rmsnorm_body(x_ref, g_ref, o_ref):
    o_ref[...] = _rms(x_ref[...], g_ref[...]).astype(o_ref.dtype)


def _rmsnorm(x, g):
    R, D = x.shape
    tm = _tile(R, 256)
    return pl.pallas_call(
        _rmsnorm_body,
        grid=(R // tm,),
        in_specs=[pl.BlockSpec((tm, D), lambda i: (i, 0)),
                  pl.BlockSpec((1, D), lambda i: (0, 0))],
        out_specs=pl.BlockSpec((tm, D), lambda i: (i, 0)),
        out_shape=jax.ShapeDtypeStruct((R, D), BF16),
        compiler_params=_cparams(1, 2 * tm * D * 6),
        name="rmsnorm",
    )(x, g.reshape(1, D))


def _ffn_body(emit_norm, nj, n_cast, x_ref, g_ref, wg_ref, wu_ref, wd_ref, *rest):
    if emit_norm:
        g2_ref, rest = rest[0], rest[1:]
    cast_in, rest = rest[:n_cast], rest[n_cast:]
    out_ref, rest = rest[0], rest[1:]
    if emit_norm:
        hn_ref, rest = rest[0], rest[1:]
    cast_out, (xn_scr,) = rest[:n_cast], rest[n_cast:]
    j = pl.program_id(1)

    tm = out_ref.shape[0]
    rc = min(tm, 64)

    @pl.when(j == 0)
    def _():
        for r0 in range(0, tm, rc):
            x = x_ref[r0:r0 + rc, :]
            xn_scr[r0:r0 + rc, :] = _rms(x, g_ref[...]).astype(BF16)
            out_ref[r0:r0 + rc, :] = x

    for src, dst in zip(cast_in, cast_out):
        dst[...] = src[...].astype(dst.dtype)

    xn = xn_scr[...]
    a = jnp.dot(xn, wg_ref[...], preferred_element_type=F32)
    b = jnp.dot(xn, wu_ref[...], preferred_element_type=F32)
    h = (a * jax.nn.sigmoid(a) * b * 0.5).astype(BF16)
    D = out_ref.shape[1]
    cw = min(D, 1024)
    for c0 in range(0, D, cw):
        out_ref[:, c0:c0 + cw] += jnp.dot(h, wd_ref[:, c0:c0 + cw], preferred_element_type=F32)

    if emit_norm:
        @pl.when(j == nj - 1)
        def _():
            for r0 in range(0, tm, rc):
                hn_ref[r0:r0 + rc, :] = _rms(out_ref[r0:r0 + rc, :], g2_ref[...]).astype(BF16)


def _grid_blocks(arr, ni, nj):
    r, c = arr.shape
    for (nr, nc, imap) in ((ni, nj, lambda i, j: (i, j)), (nj, ni, lambda i, j: (j, i))):
        if r % nr == 0 and c % nc == 0 and (r // nr) % 16 == 0 and (c // nc) % V7X_LANES == 0:
            return pl.BlockSpec((r // nr, c // nc), imap)
    return None


def _ffn(x, g, wg, wu, wd, g2=None, cast=()):
    R, D = x.shape
    Fdim = wg.shape[1]
    tm = _tile(R, 512)
    tf = _tile(Fdim, V7X_MXU_DIM)
    nj = Fdim // tf
    emit_norm = g2 is not None
    cast_specs = [_grid_blocks(c, R // tm, nj) for c in cast]
    if any(sp is None for sp in cast_specs):
        res = _ffn(x, g, wg, wu, wd, g2)
        return (tuple(res) if emit_norm else (res,)) + (None,) * len(cast)
    in_specs = [pl.BlockSpec((tm, D), lambda i, j: (i, 0), pipeline_mode=pl.Buffered(1)),
                pl.BlockSpec((1, D), lambda i, j: (0, 0)),
                pl.BlockSpec((D, tf), lambda i, j: (0, j)),
                pl.BlockSpec((D, tf), lambda i, j: (0, j)),
                pl.BlockSpec((tf, D), lambda i, j: (j, 0))]
    args = [x, g.reshape(1, D), wg, wu, wd]
    out_specs = [pl.BlockSpec((tm, D), lambda i, j: (i, 0))]
    out_shape = [jax.ShapeDtypeStruct((R, D), F32)]
    vmem = tm * D * 4 + 2 * tm * D * 4 + tm * D * 2 + 2 * 3 * D * tf * 2 + 4 * tm * tf * 4
    if emit_norm:
        in_specs.append(pl.BlockSpec((1, D), lambda i, j: (0, 0)))
        args.append(g2.reshape(1, D))
        out_specs.append(pl.BlockSpec((tm, D), lambda i, j: (i, 0)))
        out_shape.append(jax.ShapeDtypeStruct((R, D), BF16))
        vmem += 2 * tm * D * 2
    in_specs += cast_specs
    args += list(cast)
    out_specs += cast_specs
    out_shape += [jax.ShapeDtypeStruct(c.shape, BF16) for c in cast]
    vmem += sum(2 * 6 * sp.block_shape[0] * sp.block_shape[1] for sp in cast_specs)
    res = pl.pallas_call(
        functools.partial(_ffn_body, emit_norm, nj, len(cast)),
        grid=(R // tm, nj),
        in_specs=in_specs,
        out_specs=out_specs,
        out_shape=out_shape,
        scratch_shapes=[pltpu.VMEM((tm, D), BF16)],
        compiler_params=_cparams(2, vmem),
        name="ffn_norm" if emit_norm else "ffn",
    )(*args)
    if cast:
        return tuple(res)
    return res if emit_norm else res[0]


def _proj_body(epi, n_w, n_ex, a_ref, *refs):
    w_refs = refs[:n_w]
    ex_refs = refs[n_w:n_w + n_ex]
    out_refs = refs[n_w + n_ex:]
    a = a_ref[...]
    accs = [jnp.dot(a, w[...], preferred_element_type=F32) for w in w_refs]
    y = epi(accs, [e[...] for e in ex_refs])
    for o in out_refs:
        o[...] = y.astype(o.dtype)


def _proj(a, ws, epi, out_dtypes, *, cols=(), tiles=(), name, tm_pref=1024, tn_pref=512):
    R, K = a.shape
    ws = [w if isinstance(w, tuple) else (w, 0, w.shape[1]) for w in ws]
    N = ws[0][2]
    tm = _tile(R, tm_pref)
    tn = _tile(N, tn_pref)
    ws = [(w, c0 // tn) if c0 % tn == 0 else (w[:, c0:c0 + N], 0) for w, c0, _ in ws]
    in_specs = [pl.BlockSpec((tm, K), lambda i, j: (i, 0))]
    in_specs += [pl.BlockSpec((K, tn), lambda i, j, j0=j0: (0, j0 + j)) for _, j0 in ws]
    in_specs += [pl.BlockSpec((1, tn), lambda i, j: (0, j)) for _ in cols]
    in_specs += [pl.BlockSpec((tm, tn), lambda i, j: (i, j)) for _ in tiles]
    vmem = 2 * tm * K * 2 + 2 * len(ws) * K * tn * 2 + (len(ws) + 2) * tm * tn * 4
    vmem += sum(2 * tm * tn * t.dtype.itemsize for t in tiles)
    vmem += sum(2 * tm * tn * jnp.dtype(d).itemsize for d in out_dtypes)
    res = pl.pallas_call(
        functools.partial(_proj_body, epi, len(ws), len(cols) + len(tiles)),
        grid=(R // tm, N // tn),
        in_specs=in_specs,
        out_specs=[pl.BlockSpec((tm, tn), lambda i, j: (i, j)) for _ in out_dtypes],
        out_shape=[jax.ShapeDtypeStruct((R, N), d) for d in out_dtypes],
        compiler_params=_cparams(2, vmem),
        name=name,
    )(a, *[w for w, _ in ws], *cols, *tiles)
    return res


def _epi_plain(accs, ex):
    return accs[0]


def _epi_group_rms(width, accs, ex):
    return _group_rms(accs[0], ex[0], width)


def _epi_glu(accs, ex):
    return accs[0] * jax.nn.sigmoid(accs[1])


def _epi_sigmoid(accs, ex):
    return jax.nn.sigmoid(accs[0])


def _epi_residual(accs, ex):
    return ex[0] + accs[0]


def _merge_body(a_ref, c_ref, m_ref, wa_ref, wc_ref, wm_ref, ga_ref, gc_ref, gm_ref, o_ref):
    ya = jnp.dot(a_ref[...], wa_ref[...], preferred_element_type=F32)
    yc = jnp.dot(c_ref[...], wc_ref[...], preferred_element_type=F32)
    ym = jnp.dot(m_ref[...], wm_ref[...], preferred_element_type=F32)
    merged = ga_ref[...].astype(F32) * ya + gc_ref[...].astype(F32) * yc + gm_ref[...].astype(F32) * ym
    o_ref[...] = merged.astype(o_ref.dtype)


def _merge(oa, oc, om, wa, wc, wm, gates):
    R = oa.shape[0]
    D = wa.shape[1]
    tm = _tile(R, 512)
    tn = _tile(D, 512)
    nt = D // tn
    row = lambda arr: pl.BlockSpec((tm, arr.shape[1]), lambda i, j: (i, 0))
    col = lambda arr: pl.BlockSpec((arr.shape[0], tn), lambda i, j: (0, j))
    gate = lambda b: pl.BlockSpec((tm, tn), lambda i, j: (i, b * nt + j))
    ksum = oa.shape[1] + oc.shape[1] + om.shape[1]
    vmem = 2 * tm * ksum * 2 + 2 * ksum * tn * 2 + 2 * 4 * tm * tn * 2 + 4 * tm * tn * 4
    return pl.pallas_call(
        _merge_body,
        grid=(R // tm, nt),
        in_specs=[row(oa), row(oc), row(om), col(wa), col(wc), col(wm), gate(0), gate(1), gate(2)],
        out_specs=pl.BlockSpec((tm, tn), lambda i, j: (i, j)),
        out_shape=jax.ShapeDtypeStruct((R, D), BF16),
        compiler_params=_cparams(2, vmem),
        name="merge",
    )(oa, oc, om, wa, wc, wm, gates, gates, gates)


def _t5_bucket(rel):
    nb = REL_BUCKETS // 2
    max_exact = nb // 2
    n = jnp.abs(rel)
    nf = jnp.maximum(n, max_exact).astype(F32)
    large = max_exact + (jnp.log(nf / max_exact) / math.log(REL_MAX_DIST / max_exact)
                         * (nb - max_exact)).astype(jnp.int32)
    large = jnp.minimum(large, nb - 1)
    return jnp.where(rel > 0, nb, 0) + jnp.where(n < max_exact, n, large)


def _bias_window(table, q0, nq, k0, nk):
    far = table[REL_BUCKETS // 2 - 1].astype(F32)

    def toeplitz(off, n, m):
        span = n + m - 1
        rel = off - (n - 1) + np.arange(span, dtype=np.int32)
        v = ((table[_t5_bucket(jnp.asarray(rel))].astype(F32) - far) * LOG2E).T
        rows = jnp.tile(jnp.pad(v, ((0, 0), (0, 1))), (1, n))[:, :n * span].reshape(v.shape[0], n, span)
        return rows[:, :, n - 1:n - 1 + m]

    g = REL_MAX_DIST
    if nq % g or nk % g or (k0 - q0) % g:
        b = toeplitz(k0 - q0, nq, nk)
    else:
        a_idx = np.arange(nq // g)[:, None]
        b_idx = np.arange(nk // g)[None, :]
        diag = (k0 - q0) // g + b_idx - a_idx
        b = 0.0
        for c in (-1, 0, 1):
            place = jnp.asarray((diag == c).astype(np.float32))
            b = b + place[None, :, None, :, None] * toeplitz(c * g, g, g)[:, None, :, None, :]
        b = b.reshape(b.shape[0], nq, nk)
    qpos = q0 + np.arange(nq)
    kpos = k0 + np.arange(nk)
    visible = (kpos[None, :] // CHUNK) <= (qpos[:, None] // CHUNK)
    return jnp.where(jnp.asarray(visible)[None], b, NEG_INF)


def _lambda(lam_ref, lam_init):
    lv = lam_ref[...]
    s1 = jnp.sum(lv[0:1] * lv[1:2], axis=-1, keepdims=True)
    s2 = jnp.sum(lv[2:3] * lv[3:4], axis=-1, keepdims=True)
    return jnp.exp(s1) - jnp.exp(s2) + lam_init


def _attn_prompt_body(tq, tk, d, lam_init, qt_ref, k_ref, vt_ref, bias_ref, lam_ref, g_ref, o_ref,
                      s_scr, m_scr, l_scr, acc_scr):
    i = pl.program_id(1)
    r = tq // tk
    first_near = r * i - 1
    m_scr[...] = jnp.full(m_scr.shape, NEG_INF, F32)
    l_scr[...] = jnp.zeros(l_scr.shape, F32)
    acc_scr[...] = jnp.zeros(acc_scr.shape, F32)

    def scores(j, half, slot=None):
        row0 = pl.multiple_of(j * tk, tk)
        for s in range(2):
            sc = jnp.dot(k_ref[pl.ds(row0, tk), s * d:(s + 1) * d], qt_ref[0, 0, s * d:(s + 1) * d, :],
                         preferred_element_type=F32)
            if slot is not None:
                sc = sc + bias_ref[0, slot]
            s_scr[half, s] = sc

    def softmax_pv(j, half):
        vt = vt_ref[0, j]
        for s in range(2):
            sc = s_scr[half, s]
            m_old = m_scr[s]
            m_new = jnp.maximum(m_old, jnp.max(sc, axis=0, keepdims=True))
            alpha = jnp.exp2(m_old - m_new)
            p = jnp.exp2(sc - m_new)
            l_scr[s] = alpha * l_scr[s] + jnp.sum(p, axis=0, keepdims=True)
            acc_scr[s] = alpha * acc_scr[s] + jnp.dot(vt, p.astype(BF16), preferred_element_type=F32)
            m_scr[s] = m_new

    def step(j, half, slot=None):
        scores(j + 1, 1 - half, slot)
        softmax_pv(j, half)

    scores(0, 0, jnp.clip(1 - first_near, 0, r + 1))
    n_far = jnp.maximum(first_near - 1, 0)

    if r % 2 == 0:
        def far2(jp, carry):
            step(2 * jp, 0)
            step(2 * jp + 1, 1)
            return carry

        lax.fori_loop(0, n_far // 2, far2, 0)
        half_of = lambda n: n % 2
    else:
        def far1(j, carry):
            step(j, j % 2)
            return carry

        lax.fori_loop(0, n_far, far1, 0)
        half_of = lambda n: (first_near + n - 1) % 2

    for n in range(r + 1):
        @pl.when(first_near + n >= 1)
        def _(n=n):
            step(first_near + n - 1, half_of(n), 1 + n)

    softmax_pv(first_near + r, half_of(r + 1))

    lam = _lambda(lam_ref, lam_init)
    o = acc_scr[0] / l_scr[0] - lam * (acc_scr[1] / l_scr[1])
    o = o * lax.rsqrt(jnp.mean(o * o, axis=0, keepdims=True) + EPS) * g_ref[...] * (1.0 - lam_init)
    o_ref[...] = o.T.astype(o_ref.dtype)


def _attn_prompt(q, k, v, table, lam_vecs, subln, lam_init, n_heads, tq_pref=1024, tk_pref=512):
    T = q.shape[0]
    w = q.shape[1] // n_heads
    d = w // 2
    tq = _tile(T, tq_pref)
    tk = _tile(tq, tk_pref)
    r = tq // tk
    assert tk % CHUNK == 0 and tk >= REL_MAX_DIST, (tk, "key tiles two back must be at constant bias")
    qt = q.reshape(T // tq, tq, n_heads, w).transpose(2, 0, 3, 1)
    vt = v.reshape(T // tk, tk, n_heads, w).transpose(2, 0, 3, 1)
    near = [_bias_window(table, tk, tq, n * tk, tk) for n in range(r + 1)]
    bias = jnp.stack([jnp.zeros_like(near[0])] + near, axis=1).transpose(0, 1, 3, 2)
    vmem = 2 * T * w * 2 + (r + 2) * tk * tq * 4 + 4 * tq * w * 2 + 2 * tq * w * 4 + 10 * tk * tq * 4
    return pl.pallas_call(
        functools.partial(_attn_prompt_body, tq, tk, d, lam_init),
        grid=(n_heads, T // tq),
        in_specs=[pl.BlockSpec((1, 1, w, tq), lambda h, i: (h, i, 0, 0)),
                  pl.BlockSpec((T, w), lambda h, i: (0, h), pipeline_mode=pl.Buffered(1)),
                  pl.BlockSpec((1, T // tk, w, tk), lambda h, i: (h, 0, 0, 0), pipeline_mode=pl.Buffered(1)),
                  pl.BlockSpec((1, r + 2, tk, tq), lambda h, i: (h, 0, 0, 0), pipeline_mode=pl.Buffered(1)),
                  pl.BlockSpec((4, d), lambda h, i: (0, 0)),
                  pl.BlockSpec((w, 1), lambda h, i: (0, 0))],
        out_specs=pl.BlockSpec((tq, w), lambda h, i: (i, h)),
        out_shape=jax.ShapeDtypeStruct((T, n_heads * w), BF16),
        scratch_shapes=[pltpu.VMEM((2, 2, tk, tq), F32), pltpu.VMEM((2, 1, tq), F32),
                        pltpu.VMEM((2, 1, tq), F32), pltpu.VMEM((2, w, tq), F32)],
        compiler_params=_cparams(2, vmem),
        name="attn_prompt",
    )(qt, k, vt, bias, lam_vecs, subln.reshape(w, 1))


def _attn_sample_body(d, lam_init, q_ref, ck_ref, cv_ref, kn_ref, vn_ref, mb_ref, mbn_ref,
                      lam_ref, g_ref, o_ref, m_scr, l_scr, acc_scr):
    pj = pl.program_id(1)
    last = pl.num_programs(1) - 1
    nt = (((1,), (1,)), ((), ()))

    @pl.when(pj == 0)
    def _():
        m_scr[...] = jnp.full(m_scr.shape, NEG_INF, F32)
        l_scr[...] = jnp.zeros(l_scr.shape, F32)
        acc_scr[...] = jnp.zeros(acc_scr.shape, F32)

    def update(kb, vb, mb):
        for s in range(2):
            g = lax.dot_general(q_ref[0, s], kb[:, s * d:(s + 1) * d], nt, preferred_element_type=F32) + mb
            m_old = m_scr[s]
            m_new = jnp.maximum(m_old, jnp.max(g, axis=-1, keepdims=True))
            alpha = jnp.exp2(m_old - m_new)
            p = jnp.exp2(g - m_new)
            l_scr[s] = alpha * l_scr[s] + jnp.sum(p, axis=-1, keepdims=True)
            acc_scr[s] = alpha * acc_scr[s] + jnp.dot(p.astype(BF16), vb, preferred_element_type=F32)
            m_scr[s] = m_new

    slot = (pj == last).astype(jnp.int32)
    rows = ck_ref.shape[1]
    ch = min(rows, 1024)
    for r0 in range(0, rows, ch):
        update(ck_ref[0, r0:r0 + ch, :].astype(BF16), cv_ref[0, r0:r0 + ch, :].astype(BF16),
               mb_ref[slot, :, r0:r0 + ch])

    @pl.when(pj == last)
    def _():
        update(kn_ref[0], vn_ref[0], mbn_ref[...])
        lam = _lambda(lam_ref, lam_init)
        o = acc_scr[0] / l_scr[0] - lam * (acc_scr[1] / l_scr[1])
        o_ref[0] = (_rms(o, g_ref[...]) * (1.0 - lam_init)).astype(o_ref.dtype)


def _head_mask_bias(bias, n_heads):
    same = jnp.eye(n_heads, dtype=bool)[:, None, None, :]
    full = jnp.where(same, bias[..., None], NEG_INF)
    return full.reshape(bias.shape[0] * bias.shape[1], bias.shape[2] * n_heads)


def _attn_sample(q, kn, vn, cache_k, cache_v, table, lam_vecs, subln, lam_init, n_heads):
    B, S, HW = q.shape
    P = cache_k.shape[1]
    w = HW // n_heads
    d = w // 2
    tp = _tile(P, 256)
    hs = n_heads * S
    assert tp >= REL_MAX_DIST or tp == P, "only the newest cache tile may carry a non-constant bias"
    near = _bias_window(table, P, S, P - tp, tp)
    mb = jnp.stack([_head_mask_bias(jnp.zeros_like(near), n_heads), _head_mask_bias(near, n_heads)])
    mbn = _head_mask_bias(_bias_window(table, P, S, P, S), n_heads)
    qs = q.reshape(B, S, n_heads, 2, d).transpose(0, 3, 2, 1, 4).reshape(B, 2, hs, d)
    new = pl.BlockSpec((1, S * n_heads, w), lambda b, j: (b, 0, 0))
    old = pl.BlockSpec((1, tp * n_heads, w), lambda b, j: (b, j, 0))
    vmem = (2 * 2 * tp * HW * 4 + 2 * tp * HW * 2 + 2 * hs * tp * n_heads * 4 + 4 * hs * tp * n_heads * 4
            + 4 * hs * w * 4)
    o = pl.pallas_call(
        functools.partial(_attn_sample_body, d, lam_init),
        grid=(B, P // tp),
        in_specs=[pl.BlockSpec((1, 2, hs, d), lambda b, j: (b, 0, 0, 0)),
                  old, old, new, new,
                  pl.BlockSpec((2, hs, tp * n_heads), lambda b, j: (0, 0, 0), pipeline_mode=pl.Buffered(1)),
                  pl.BlockSpec((hs, S * n_heads), lambda b, j: (0, 0)),
                  pl.BlockSpec((4, d), lambda b, j: (0, 0)),
                  pl.BlockSpec((1, w), lambda b, j: (0, 0))],
        out_specs=pl.BlockSpec((1, hs, w), lambda b, j: (b, 0, 0)),
        out_shape=jax.ShapeDtypeStruct((B, hs, w), BF16),
        scratch_shapes=[pltpu.VMEM((2, hs, 1), F32), pltpu.VMEM((2, hs, 1), F32), pltpu.VMEM((2, hs, w), F32)],
        compiler_params=_cparams(2, vmem),
        name="attn_sample",
    )(qs, cache_k.reshape(B, P * n_heads, w), cache_v.reshape(B, P * n_heads, w),
      kn.reshape(B, S * n_heads, w), vn.reshape(B, S * n_heads, w), mb, mbn, lam_vecs, subln.reshape(1, w))
    return o.reshape(B, n_heads, S, w).transpose(0, 2, 1, 3).reshape(B, S, HW)


def _conv_body(width, hist_ref, halo_ref, main_ref, w_ref, b_ref, g_ref, beta_ref, o_ref,
               win_scr, sh_scr, conv_scr):
    i = pl.program_id(1)
    tm = main_ref.shape[1]
    C = main_ref.shape[2]

    @pl.when(i == 0)
    def _():
        win_scr[0:CONV_HALO, :] = hist_ref[0]

    @pl.when(i > 0)
    def _():
        win_scr[0:CONV_HALO, :] = halo_ref[0]

    win_scr[CONV_HALO:CONV_HALO + tm, :] = main_ref[0]
    first = CONV_HALO - (width - 1)
    for ph in range(min(8, width)):
        rows = tm + 8 * ((width - 1 - ph) // 8)
        sh_scr[ph, 0:rows, :] = win_scr[first + ph:first + ph + rows, :]
    rc = min(tm, 32)
    cc = min(C, 512)
    for r0 in range(0, tm, rc):
        for c0 in range(0, C, cc):
            acc = jnp.broadcast_to(b_ref[:, c0:c0 + cc], (rc, cc))
            for tap in range(width):
                ph, off = tap % 8, r0 + 8 * (tap // 8)
                acc = acc + sh_scr[ph, off:off + rc, c0:c0 + cc] * w_ref[tap:tap + 1, c0:c0 + cc]
            conv_scr[r0:r0 + rc, c0:c0 + cc] = acc
    c = conv_scr[...]
    xc = c - jnp.mean(c, axis=-1, keepdims=True)
    y = xc * lax.rsqrt(jnp.mean(xc * xc, axis=-1, keepdims=True) + EPS) * g_ref[...] + beta_ref[...]
    o_ref[0] = (y * jax.nn.sigmoid(y)).astype(o_ref.dtype)


def _conv_module(u, hist, w, b, g, beta):
    B, T, C = u.shape
    width = w.shape[0]
    tm = _tile(T, 64)
    nt = T // tm
    assert width - 1 <= CONV_HALO and (tm % CONV_HALO == 0 or nt == 1)
    hist_pad = jnp.concatenate([jnp.zeros((B, CONV_HALO - (width - 1), C), F32), hist], axis=1)
    halo_src = u if nt > 1 else hist_pad
    halo_per_tile = tm // CONV_HALO if nt > 1 else 0
    vec = lambda: pl.BlockSpec((1, C), lambda bb, i: (0, 0))
    sh_rows = tm + 8 * ((width - 1) // 8)
    return pl.pallas_call(
        functools.partial(_conv_body, width),
        grid=(B, nt),
        in_specs=[pl.BlockSpec((1, CONV_HALO, C), lambda bb, i: (bb, 0, 0)),
                  pl.BlockSpec((1, CONV_HALO, C), lambda bb, i: (bb, jnp.maximum(i * halo_per_tile - 1, 0), 0)),
                  pl.BlockSpec((1, tm, C), lambda bb, i: (bb, i, 0)),
                  pl.BlockSpec((width, C), lambda bb, i: (0, 0)),
                  vec(), vec(), vec()],
        out_specs=pl.BlockSpec((1, tm, C), lambda bb, i: (bb, i, 0)),
        out_shape=jax.ShapeDtypeStruct((B, T, C), BF16),
        scratch_shapes=[pltpu.VMEM((CONV_HALO + tm, C), F32), pltpu.VMEM((min(8, width), sh_rows, C), F32),
                        pltpu.VMEM((tm, C), F32)],
        compiler_params=_cparams(2, (16 * (CONV_HALO + tm) + 4 * 8 * sh_rows) * C),
        name="conv_module",
    )(hist_pad, halo_src, u, w, b.reshape(1, C), g.reshape(1, C), beta.reshape(1, C))


def _mem_attn_body(n_heads, dq, dv, q_ref, k_ref, v_ref, o_ref):
    nt = (((1,), (1,)), ((), ()))
    outs = []
    for h in range(n_heads):
        qh = q_ref[0, :, h * dq:(h + 1) * dq]
        kh = k_ref[0, :, h * dq:(h + 1) * dq].astype(BF16)
        vh = v_ref[0, :, h * dv:(h + 1) * dv].astype(BF16)
        s = lax.dot_general(qh, kh, nt, preferred_element_type=F32)
        p = jnp.exp2(s - jnp.max(s, axis=-1, keepdims=True))
        l = jnp.sum(p, axis=-1, keepdims=True)
        outs.append(jnp.dot(p.astype(BF16), vh, preferred_element_type=F32) / l)
    o_ref[0] = jnp.concatenate(outs, axis=-1).astype(o_ref.dtype)


def _mem_attn(mq, mem_k, mem_v, n_heads):
    B, T, QW = mq.shape
    M = mem_k.shape[1]
    VW = mem_v.shape[2]
    tm = _tile(T, 512)
    return pl.pallas_call(
        functools.partial(_mem_attn_body, n_heads, QW // n_heads, VW // n_heads),
        grid=(B, T // tm),
        in_specs=[pl.BlockSpec((1, tm, QW), lambda b, i: (b, i, 0)),
                  pl.BlockSpec((1, M, QW), lambda b, i: (b, 0, 0)),
                  pl.BlockSpec((1, M, VW), lambda b, i: (b, 0, 0))],
        out_specs=pl.BlockSpec((1, tm, VW), lambda b, i: (b, i, 0)),
        out_shape=jax.ShapeDtypeStruct((B, T, VW), BF16),
        compiler_params=_cparams(2, 2 * (tm * QW * 2 + M * (QW + VW) * 4 + tm * VW * 2) + 3 * tm * VW * 4),
        name="mem_attn",
    )(mq, mem_k, mem_v)


def _layer_group(x, past, mem_k, mem_v, table, lam_init, p, dims):
    B, T, D = x.shape
    H, d, MH = dims["H"], dims["d"], dims["MH"]
    R = B * T
    xf = x.reshape(R, D)

    if "ffn2_w" not in p:
        x1, h, *w2 = _ffn(xf, p["ffn1_norm"], p["ffn1_w_gate"], p["ffn1_w_up"], p["ffn1_w_down"], p["mix_norm"],
                          cast=p["ffn2_w_f32"])
        p["ffn2_w"] = [w.astype(BF16) for w in p["ffn2_w_f32"]] if w2[0] is None else w2
    else:
        x1, h = _ffn(xf, p["ffn1_norm"], p["ffn1_w_gate"], p["ffn1_w_up"], p["ffn1_w_down"], p["mix_norm"])

    (q,) = _proj(h, [p["w_q"]], functools.partial(_epi_group_rms, d), [BF16], cols=[p["q_gain"]], name="proj_q")
    k32, k16 = _proj(h, [p["w_k"]], functools.partial(_epi_group_rms, d), [F32, BF16], cols=[p["k_gain"]],
                     name="proj_k")
    v32, v16 = _proj(h, [p["w_v"]], _epi_plain, [F32, BF16], name="proj_v")
    (u,) = _proj(h, [p["w_glu_a"], p["w_glu_b"]], _epi_glu, [F32], name="proj_glu")
    (mq,) = _proj(h, [p["w_mq"]], functools.partial(_epi_group_rms, dims["dmq"]), [BF16], cols=[p["mq_gain"]],
                  name="proj_mq")
    (gates,) = _proj(h, [p["w_gates"]], _epi_sigmoid, [BF16], name="proj_gates")

    HW = H * 2 * d
    C = u.shape[1]
    width = p["conv_dw"].shape[0]
    if past is None:
        assert B == 1
        oa = _attn_prompt(q, k16, v16, table, p["lam_vecs"], p["attn_subln"], lam_init, H)
        hist = jnp.zeros((B, width - 1, C), F32)
    else:
        cache_k, cache_v, hist = past
        oa = _attn_sample(q.reshape(B, T, HW), k16.reshape(B, T, HW), v16.reshape(B, T, HW), cache_k, cache_v,
                          table, p["lam_vecs"], p["attn_subln"], lam_init, H).reshape(R, HW)

    u3 = u.reshape(B, T, C)
    conv_state = (u3[:, T - (width - 1):] if T >= width - 1
                  else jnp.concatenate([hist[:, T:], u3], axis=1))
    oc = _conv_module(u3, hist, p["conv_dw"], p["conv_dw_bias"], p["conv_ln_g"], p["conv_ln_b"]).reshape(R, C)

    M = mem_k.shape[1]
    om = _mem_attn(mq.reshape(B, T, -1), mem_k.reshape(B, M, -1), mem_v.reshape(B, M, -1), MH)
    om = om.reshape(R, -1)

    merged = _merge(oa, oc, om, p["w_attn_out"], p["w_conv_out"], p["w_mem_out"], gates)
    (x2,) = _proj(merged, [p["w_out"]], _epi_residual, [F32], tiles=[x1], name="proj_out")
    y = _ffn(x2, p["ffn2_norm"], *p["ffn2_w"])
    return (y.reshape(B, T, D), k32.reshape(B, T, H, 2 * d), v32.reshape(B, T, H, 2 * d), conv_state)


def _memory_kv(mem, mem_norm, w_k, w_v, k_gain, dims):
    B, M, D = mem.shape
    hm = _rmsnorm(mem.reshape(B * M, D), mem_norm)
    (k,) = _proj(hm, [w_k], functools.partial(_epi_group_rms, dims["dmq"]), [F32], cols=[k_gain], name="mem_k")
    (v,) = _proj(hm, [w_v], _epi_plain, [F32], name="mem_v")
    MH = dims["MH"]
    return k.reshape(B, M, MH, -1), v.reshape(B, M, MH, -1)


def kernel(x_prompt, x_sample, cache_attn_k, cache_attn_v, cache_conv, cache_mem_k, cache_mem_v, mem_prompt, rel_bias_table, ffn1_norm, ffn1_w_gate, ffn1_w_up, ffn1_w_down, mix_norm, w_in, attn_q_norm, attn_k_norm, lambda_q1, lambda_k1, lambda_q2, lambda_k2, attn_subln, w_attn_out, conv_dw, conv_dw_bias, conv_ln_g, conv_ln_b, w_conv_out, mem_norm, w_mem_kv, mem_q_norm, mem_k_norm, w_mem_out, w_out, ffn2_norm, ffn2_w_gate, ffn2_w_up, ffn2_w_down):
    depth = w_in.shape[0]
    D = x_prompt.shape[-1]
    H = cache_attn_k.shape[3]
    d = attn_q_norm.shape[-1]
    C = conv_dw.shape[-1]
    MH = cache_mem_k.shape[3]
    dmq = cache_mem_k.shape[4]
    dims = {"H": H, "d": d, "MH": MH, "dmq": dmq}
    c_q = H * 2 * d
    c_k, c_v = 2 * c_q, 3 * c_q
    c_conv = c_v + 2 * C
    c_memq = c_conv + MH * dmq
    bf = lambda a: a.astype(BF16)

    yp, ys = x_prompt, x_sample
    outs = [[] for _ in range(8)]
    for l in range(depth):
        lam_init = 0.8 - 0.6 * math.exp(-0.3 * l)
        wi = bf(w_in[l])
        p = {
            "ffn1_norm": ffn1_norm[l], "ffn1_w_gate": bf(ffn1_w_gate[l]), "ffn1_w_up": bf(ffn1_w_up[l]),
            "ffn1_w_down": bf(ffn1_w_down[l]), "mix_norm": mix_norm[l],
            "w_q": (wi, 0, c_q), "w_k": (wi, c_q, c_q), "w_v": (wi, c_k, c_q),
            "w_glu_a": (wi, c_v, C), "w_glu_b": (wi, c_v + C, C),
            "w_mq": (wi, c_conv, MH * dmq), "w_gates": (wi, c_memq, wi.shape[1] - c_memq),
            "q_gain": jnp.tile(attn_q_norm[l] * (d ** -0.5 * LOG2E), 2 * H).reshape(1, c_q),
            "k_gain": jnp.tile(attn_k_norm[l], 2 * H).reshape(1, c_q),
            "mq_gain": jnp.tile(mem_q_norm[l] * (dmq ** -0.5 * LOG2E), MH).reshape(1, MH * dmq),
            "lam_vecs": jnp.stack([lambda_q1[l], lambda_k1[l], lambda_q2[l], lambda_k2[l]]).astype(F32),
            "attn_subln": attn_subln[l], "w_attn_out": bf(w_attn_out[l]),
            "conv_dw": conv_dw[l], "conv_dw_bias": conv_dw_bias[l],
            "conv_ln_g": conv_ln_g[l], "conv_ln_b": conv_ln_b[l], "w_conv_out": bf(w_conv_out[l]),
            "w_mem_out": bf(w_mem_out[l]), "w_out": bf(w_out[l]),
            "ffn2_norm": ffn2_norm[l], "ffn2_w_f32": (ffn2_w_gate[l], ffn2_w_up[l], ffn2_w_down[l]),
        }
        wkv = w_mem_kv[l]
        mk, mv = _memory_kv(mem_prompt, mem_norm[l], bf(wkv[:, :MH * dmq]), bf(wkv[:, MH * dmq:]),
                            jnp.tile(mem_k_norm[l], MH).reshape(1, MH * dmq), dims)
        yp, kp, vp, cp = _layer_group(yp, None, mk, mv, rel_bias_table, lam_init, p, dims)
        ys, ks, vs, cs = _layer_group(ys, (cache_attn_k[l], cache_attn_v[l], cache_conv[l]),
                                      cache_mem_k[l], cache_mem_v[l], rel_bias_table, lam_init, p, dims)
        for lst, val in zip(outs, (kp, vp, cp, mk, mv, ks, vs, cs)):
            lst.append(val)
    return (yp, ys) + tuple(jnp.stack(o) for o in outs)
```
